```python
import math
import jax, jax.numpy as jnp
from jax import lax
import numpy as np

D_MODEL = 1024
BATCH = 8
SEQ = 4096
DEPTH = 2

N_MIXERS = 2
N_SGU_LAYERS = (DEPTH + 1) // 2
N_SWA_LAYERS = DEPTH // 2

CHUNK = 128
SGU_WIDTH = 2 * D_MODEL
SGU_GROUPS = 16
SGU_GROUP_DIM = SGU_WIDTH // SGU_GROUPS

HEAD_DIM = 64
N_HEADS = D_MODEL // HEAD_DIM
N_KV_HEADS = 4
KV_GROUP = N_HEADS // N_KV_HEADS
WINDOW = 128
BLOCK = WINDOW

REL_BUCKETS = 32
REL_MAX_DIST = 128

D_FF = 2816
CONV_WIDTH = 3

EPS = 1e-6

kernel_name = "hybrid_sgu_swa_convffn"


def rms_norm(x, gain):
    xf = x.astype(jnp.float32)
    y = xf * lax.rsqrt(jnp.mean(xf * xf, axis=-1, keepdims=True) + EPS)
    return (y * gain.astype(jnp.float32)).astype(x.dtype)


def _rel_buckets():
    q = np.arange(BLOCK)[:, None] + BLOCK
    k = np.arange(2 * BLOCK)[None, :]
    dist = q - k
    n = np.maximum(dist, 0)
    max_exact = REL_BUCKETS // 2
    large = max_exact + (np.log(np.maximum(n, 1).astype(np.float32) / max_exact)
                         / math.log(REL_MAX_DIST / max_exact)
                         * (REL_BUCKETS - max_exact)).astype(np.int32)
    large = np.minimum(large, REL_BUCKETS - 1)
    bucket = np.where(n < max_exact, n, large).astype(np.int32)
    return dist, bucket


def sgu_mixer(h, w_in, v_gain, w_s, b_s, w_out):
    B, S, _ = h.shape
    nc = S // CHUNK
    z = jax.nn.gelu(h @ w_in, approximate=False)
    u, v = jnp.split(z, 2, axis=-1)
    v = rms_norm(v, v_gain).reshape(B, nc, CHUNK, SGU_GROUPS, SGU_GROUP_DIM)
    w_causal = jnp.tril(w_s)
    s = jnp.einsum('gts,bcsgd->bctgd', w_causal, v) + b_s.T[:, :, None]
    return (u * s.reshape(B, S, SGU_WIDTH)) @ w_out


def swa_mixer(h, w_qkv, q_gain, k_gain, sinks, w_o, rel_bias):
    B, S, _ = h.shape
    nb = S // BLOCK
    qkv = h @ w_qkv
    q, k, v = jnp.split(qkv, [N_HEADS * HEAD_DIM, (N_HEADS + N_KV_HEADS) * HEAD_DIM], axis=-1)
    q = rms_norm(q.reshape(B, S, N_HEADS, HEAD_DIM), q_gain)
    k = rms_norm(k.reshape(B, S, N_KV_HEADS, HEAD_DIM), k_gain)
    v = v.reshape(B, S, N_KV_HEADS, HEAD_DIM)
    q = q.reshape(B, nb, BLOCK, N_KV_HEADS, KV_GROUP, HEAD_DIM)

    def band(t):
        t = t.reshape(B, nb, BLOCK, N_KV_HEADS, HEAD_DIM)
        prev = jnp.pad(t, ((0, 0), (1, 0), (0, 0), (0, 0), (0, 0)))[:, :-1]
        return jnp.concatenate([prev, t], axis=2)

    kb, vb = band(k), band(v)
    scores = jnp.einsum('bnqhgd,bnkhd->bnhgqk', q, kb).astype(jnp.float32) * (HEAD_DIM ** -0.5)

    dist, bucket = _rel_buckets()
    bias = rel_bias.astype(jnp.float32)[bucket]
    bias = jnp.transpose(bias, (2, 0, 1)).reshape(N_KV_HEADS, KV_GROUP, BLOCK, 2 * BLOCK)
    scores = scores + bias

    blk = np.arange(nb)[:, None, None]
    kk = np.arange(2 * BLOCK)[None, None, :]
    valid = ((dist >= 0) & (dist < WINDOW))[None] & ((blk > 0) | (kk >= BLOCK))
    scores = jnp.where(valid[None, :, None, None], scores, -jnp.inf)

    sink = sinks.astype(jnp.float32).reshape(N_KV_HEADS, KV_GROUP)[None, None, :, :, None, None]
    m = jnp.maximum(jnp.max(scores, axis=-1, keepdims=True), sink)
    p = jnp.exp(scores - m)
    denom = jnp.sum(p, axis=-1, keepdims=True) + jnp.exp(sink - m)
    p = (p / denom).astype(vb.dtype)
    o = jnp.einsum('bnhgqk,bnkhd->bnqhgd', p, vb).reshape(B, S, N_HEADS * HEAD_DIM)
    return o @ w_o


def conv_glu_ffn(h, w_up, conv_w, conv_b, w_down):
    a = h @ w_up
    C = a.shape[-1]
    c = lax.conv_general_dilated(a, conv_w[:, None, :].astype(a.dtype), window_strides=(1,),
                                 padding=[(CONV_WIDTH - 1, 0)],
                                 dimension_numbers=('NWC', 'WIO', 'NWC'),
                                 feature_group_count=C) + conv_b
    g, val = jnp.split(c, 2, axis=-1)
    return (jax.nn.silu(g) * val) @ w_down


def setup_inputs(seed: int = 0) -> dict:
    key = jax.random.key(seed)
    ks = jax.random.split(key, 20)
    f32 = jnp.float32
    nrm = lambda k, shape, scale: jax.random.normal(k, shape, f32) * scale
    return {
        "x": nrm(ks[0], (BATCH, SEQ, D_MODEL), 1.0),
        "mix_norm": 1.0 + nrm(ks[1], (DEPTH, D_MODEL), 0.05),
        "ffn_norm": 1.0 + nrm(ks[2], (DEPTH, D_MODEL), 0.05),
        "sgu_w_in": nrm(ks[3], (N_SGU_LAYERS, D_MODEL, 2 * SGU_WIDTH), D_MODEL ** -0.5),
        "sgu_v_gain": 1.0 + nrm(ks[4], (N_SGU_LAYERS, SGU_WIDTH), 0.05),
        "sgu_w_s": nrm(ks[5], (N_SGU_LAYERS, SGU_GROUPS, CHUNK, CHUNK), CHUNK ** -0.5),
        "sgu_b_s": 1.0 + nrm(ks[6], (N_SGU_LAYERS, SGU_GROUPS, CHUNK), 0.05),
        "sgu_w_out": nrm(ks[7], (N_SGU_LAYERS, SGU_WIDTH, D_MODEL), SGU_WIDTH ** -0.5),
        "attn_w_qkv": nrm(ks[8], (N_SWA_LAYERS, D_MODEL, (N_HEADS + 2 * N_KV_HEADS) * HEAD_DIM), D_MODEL ** -0.5),
        "attn_q_gain": 1.0 + nrm(ks[9], (N_SWA_LAYERS, HEAD_DIM), 0.05),
        "attn_k_gain": 1.0 + nrm(ks[10], (N_SWA_LAYERS, HEAD_DIM), 0.05),
        "attn_sinks": nrm(ks[11], (N_SWA_LAYERS, N_HEADS), 0.5),
        "attn_w_o": nrm(ks[12], (N_SWA_LAYERS, N_HEADS * HEAD_DIM, D_MODEL), (N_HEADS * HEAD_DIM) ** -0.5),
        "rel_bias": nrm(ks[13], (REL_BUCKETS, N_HEADS), 0.5),
        "ffn_w_up": nrm(ks[14], (DEPTH, D_MODEL, 2 * D_FF), D_MODEL ** -0.5),
        "ffn_conv_w": nrm(ks[15], (DEPTH, CONV_WIDTH, 2 * D_FF), CONV_WIDTH ** -0.5),
        "ffn_conv_b": nrm(ks[16], (DEPTH, 2 * D_FF), 0.02),
        "ffn_w_down": nrm(ks[17], (DEPTH, D_FF, D_MODEL), D_FF ** -0.5),
    }


def reference(x, mix_norm, ffn_norm, sgu_w_in, sgu_v_gain, sgu_w_s, sgu_b_s, sgu_w_out,
              attn_w_qkv, attn_q_gain, attn_k_gain, attn_sinks, attn_w_o, rel_bias,
              ffn_w_up, ffn_conv_w, ffn_conv_b, ffn_w_down):
    h = x
    for i in range(DEPTH):
        j = i // N_MIXERS
        hn = rms_norm(h, mix_norm[i])
        if i % N_MIXERS == 0:
            h = h + sgu_mixer(hn, sgu_w_in[j], sgu_v_gain[j], sgu_w_s[j], sgu_b_s[j], sgu_w_out[j])
        else:
            h = h + swa_mixer(hn, attn_w_qkv[j], attn_q_gain[j], attn_k_gain[j], attn_sinks[j],
                              attn_w_o[j], rel_bias)
        h = h + conv_glu_ffn(rms_norm(h, ffn_norm[i]), ffn_w_up[i], ffn_conv_w[i], ffn_conv_b[i], ffn_w_down[i])
    return h
```

```python
import functools
import math

import numpy as np
import jax
import jax.numpy as jnp
from jax import lax
from jax.experimental import pallas as pl
from jax.experimental.pallas import tpu as pltpu

F32 = jnp.float32
BF16 = jnp.bfloat16

EPS = 1e-6
LANES = 128
SUBLANES = 8
CHUNK = 128
HEAD_DIM = 64
N_HEADS = 16
N_KV_HEADS = 4
KV_GROUP = N_HEADS // N_KV_HEADS
SGU_GROUPS = 16
REL_BUCKETS = 32
REL_MAX_DIST = 128
CONV_WIDTH = 3

TOKEN_TILE = 512
SGU_COL_TILE = 512
FFN_COL_TILE = 256
VMEM_LIMIT_BYTES = 56 * 1024 * 1024


def _rms_norm(x, gain):
    ms = jnp.mean(x * x, axis=-1, keepdims=True)
    return x * lax.rsqrt(ms + EPS) * gain


def _const_spec(shape):
    zeros = (0,) * len(shape)
    return pl.BlockSpec(shape, lambda b, j: zeros, pipeline_mode=pl.Buffered(1))


def _token_spec(tm, d):
    return pl.BlockSpec((None, tm, d), lambda b, j: (b, j, 0))


def _compiler_params():
    return pltpu.CompilerParams(
        dimension_semantics=("arbitrary", "arbitrary"),
        vmem_limit_bytes=VMEM_LIMIT_BYTES,
    )


def _sgu_kernel(h_ref, mg_ref, win_ref, vg_ref, ws_ref, bs_ref, wout_ref, o_ref,
                u_scr, v_scr, us_scr, wsm_scr):
    tm = h_ref.shape[0]
    width = u_scr.shape[1]

    @pl.when((pl.program_id(0) == 0) & (pl.program_id(1) == 0))
    def _():
        row = lax.broadcasted_iota(jnp.int32, (CHUNK, CHUNK), 0)
        col = lax.broadcasted_iota(jnp.int32, (CHUNK, CHUNK), 1)
        for g in range(SGU_GROUPS):
            wsm_scr[g] = jnp.where(row >= col, ws_ref[g], 0.0).astype(BF16)

    h = h_ref[...]
    hn = _rms_norm(h, mg_ref[...]).astype(BF16)

    ssq = jnp.zeros((tm, 1), F32)
    n_col = (2 * width) // SGU_COL_TILE
    for c in range(n_col):
        c0 = c * SGU_COL_TILE
        z = jnp.dot(hn, win_ref[:, c0:c0 + SGU_COL_TILE], preferred_element_type=F32)
        z = 0.5 * z * (1.0 + lax.erf(z * np.float32(math.sqrt(0.5))))
        if c0 < width:
            u_scr[:, c0:c0 + SGU_COL_TILE] = z
        else:
            ssq = ssq + jnp.sum(z * z, axis=-1, keepdims=True)
            v_scr[:, c0 - width:c0 - width + SGU_COL_TILE] = z
    rstd = lax.rsqrt(ssq * np.float32(1.0 / width) + EPS)

    for cb in range(tm // CHUNK):
        r0 = cb * CHUNK
        rs = rstd[r0:r0 + CHUNK]
        for g in range(SGU_GROUPS):
            g0 = g * LANES
            vn = (v_scr[r0:r0 + CHUNK, g0:g0 + LANES] * rs * vg_ref[:, g0:g0 + LANES]).astype(BF16)
            s = jnp.dot(wsm_scr[g], vn, preferred_element_type=F32) + bs_ref[:, g:g + 1]
            us_scr[r0:r0 + CHUNK, g0:g0 + LANES] = (u_scr[r0:r0 + CHUNK, g0:g0 + LANES] * s).astype(BF16)

    o_ref[...] = h + jnp.dot(us_scr[...], wout_ref[...], preferred_element_type=F32)


def _sgu_layer(h, mix_gain, w_in, v_gain, w_s, b_s, w_out):
    B, S, D = h.shape
    tm = min(TOKEN_TILE, S)
    width = w_out.shape[0]
    return pl.pallas_call(
        _sgu_kernel,
        grid=(B, S // tm),
        in_specs=[
            _token_spec(tm, D),
            _const_spec((1, D)),
            _const_spec((D, 2 * width)),
            _const_spec((1, width)),
            _const_spec((SGU_GROUPS, CHUNK, CHUNK)),
            _const_spec((CHUNK, SGU_GROUPS)),
            _const_spec((width, D)),
        ],
        out_specs=_token_spec(tm, D),
        out_shape=jax.ShapeDtypeStruct((B, S, D), F32),
        scratch_shapes=[
            pltpu.VMEM((tm, width), F32),
            pltpu.VMEM((tm, width), F32),
            pltpu.VMEM((tm, width), BF16),
            pltpu.VMEM((SGU_GROUPS, CHUNK, CHUNK), BF16),
        ],
        compiler_params=_compiler_params(),
        name="sgu_mixer",
    )(h, mix_gain.reshape(1, D), w_in.astype(BF16), v_gain.reshape(1, width), w_s,
      b_s.T, w_out.astype(BF16))


def _ffn_kernel(h_ref, ng_ref, wup_ref, cw_ref, cb_ref, wdown_ref, o_ref,
                carry_scr, abuf_scr, act_scr):
    tm = h_ref.shape[0]
    d_ff = act_scr.shape[1]
    ct = FFN_COL_TILE

    @pl.when(pl.program_id(1) == 0)
    def _():
        carry_scr[...] = jnp.zeros_like(carry_scr)

    h = h_ref[...]
    hn = _rms_norm(h, ng_ref[...]).astype(BF16)

    def conv_cols(col, slot):
        s0 = slot * ct
        a = jnp.dot(hn, wup_ref[:, col:col + ct], preferred_element_type=F32)
        abuf_scr[0:SUBLANES, s0:s0 + ct] = carry_scr[:, col:col + ct]
        abuf_scr[SUBLANES:SUBLANES + tm, s0:s0 + ct] = a
        carry_scr[:, col:col + ct] = a[tm - SUBLANES:tm]
        a1 = abuf_scr[SUBLANES - 1:SUBLANES - 1 + tm, s0:s0 + ct]
        a2 = abuf_scr[SUBLANES - 2:SUBLANES - 2 + tm, s0:s0 + ct]
        return (cw_ref[0:1, col:col + ct] * a2 + cw_ref[1:2, col:col + ct] * a1
                + cw_ref[2:3, col:col + ct] * a + cb_ref[:, col:col + ct])

    for c in range(d_ff // ct):
        gate = conv_cols(c * ct, 0)
        val = conv_cols(d_ff + c * ct, 1)
        act = gate / (1.0 + jnp.exp(-gate)) * val
        act_scr[:, c * ct:(c + 1) * ct] = act.astype(BF16)

    o_ref[...] = h + jnp.dot(act_scr[...], wdown_ref[...], preferred_element_type=F32)


def _ffn_layer(h, norm_gain, w_up, conv_w, conv_b, w_down):
    B, S, D = h.shape
    tm = min(TOKEN_TILE, S)
    d_ff = w_down.shape[0]
    return pl.pallas_call(
        _ffn_kernel,
        grid=(B, S // tm),
        in_specs=[
            _token_spec(tm, D),
            _const_spec((1, D)),
            _const_spec((D, 2 * d_ff)),
            _const_spec((CONV_WIDTH, 2 * d_ff)),
            _const_spec((1, 2 * d_ff)),
            _const_spec((d_ff, D)),
        ],
        out_specs=_token_spec(tm, D),
        out_shape=jax.ShapeDtypeStruct((B, S, D), F32),
        scratch_shapes=[
            pltpu.VMEM((SUBLANES, 2 * d_ff), F32),
            pltpu.VMEM((SUBLANES + tm, 2 * FFN_COL_TILE), F32),
            pltpu.VMEM((tm, d_ff), BF16),
        ],
        compiler_params=_compiler_params(),
        name="conv_glu_ffn",
    )(h, norm_gain.reshape(1, D), w_up.astype(BF16), conv_w, conv_b.reshape(1, 2 * d_ff),
      w_down.astype(BF16))


def _merged_rel_buckets():
    i = np.arange(CHUNK)[:, None]
    j = np.arange(CHUNK)[None, :]
    n = np.where(j <= i, i - j, CHUNK + i - j)
    max_exact = REL_BUCKETS // 2
    large = max_exact + (np.log(np.maximum(n, 1).astype(np.float32) / max_exact)
                         / math.log(REL_MAX_DIST / max_exact)
                         * (REL_BUCKETS - max_exact)).astype(np.int32)
    large = np.minimum(large, REL_BUCKETS - 1)
    return np.where(n < max_exact, n, large).astype(np.int32)


def _bias_kernel(bucket_ref, rb_ref, o_ref):
    head = pl.program_id(0)
    bucket = bucket_ref[...]
    acc = jnp.zeros(bucket.shape, F32)
    for b in range(REL_BUCKETS):
        acc = jnp.where(bucket == b, rb_ref[b, head], acc)
    o_ref[...] = acc


def _bias_table(rel_bias):
    bucket = jnp.asarray(_merged_rel_buckets())
    return pl.pallas_call(
        _bias_kernel,
        grid=(N_HEADS,),
        in_specs=[
            pl.BlockSpec((CHUNK, CHUNK), lambda h: (0, 0)),
            pl.BlockSpec(memory_space=pltpu.SMEM),
        ],
        out_specs=pl.BlockSpec((None, CHUNK, CHUNK), lambda h: (h, 0, 0)),
        out_shape=jax.ShapeDtypeStruct((N_HEADS, CHUNK, CHUNK), F32),
        name="rel_bias_table",
    )(bucket, rel_bias)


def _head_sumsq(x, seg):
    sq = x * x
    hi = sq.astype(BF16)
    lo = (sq - hi.astype(F32)).astype(BF16)
    return (jnp.dot(hi, seg, preferred_element_type=F32)
            + jnp.dot(lo, seg, preferred_element_type=F32))


def _swa_kernel(h_ref, mg_ref, wqkv_ref, qg_ref, kg_ref, sink_ref, bias_ref, seg_ref, wo_ref, o_ref,
                k_scr, v_scr, q_scr, att_scr):
    tm = h_ref.shape[0]
    dq = N_HEADS * HEAD_DIM
    dkv = N_KV_HEADS * HEAD_DIM
    first_tile = pl.program_id(1) == 0

    @pl.when(first_tile)
    def _():
        k_scr[0:CHUNK, :] = jnp.zeros((CHUNK, dkv), BF16)
        v_scr[0:CHUNK, :] = jnp.zeros((CHUNK, dkv), BF16)

    h = h_ref[...]
    hn = _rms_norm(h, mg_ref[...]).astype(BF16)
    seg = seg_ref[...]
    inv_hd = np.float32(1.0 / HEAD_DIM)

    for c in range(dq // LANES):
        c0 = c * LANES
        q = jnp.dot(hn, wqkv_ref[:, c0:c0 + LANES], preferred_element_type=F32)
        r = lax.rsqrt(_head_sumsq(q, seg) * inv_hd + EPS)
        q_scr[:, c0:c0 + LANES] = (q * r * qg_ref[:, c0:c0 + LANES] * np.float32(HEAD_DIM ** -0.5)).astype(BF16)
    for c in range(dkv // LANES):
        c0 = c * LANES
        k = jnp.dot(hn, wqkv_ref[:, dq + c0:dq + c0 + LANES], preferred_element_type=F32)
        r = lax.rsqrt(_head_sumsq(k, seg) * inv_hd + EPS)
        k_scr[CHUNK:CHUNK + tm, c0:c0 + LANES] = (k * r * kg_ref[:, c0:c0 + LANES]).astype(BF16)
    v_scr[CHUNK:CHUNK + tm, :] = jnp.dot(
        hn, wqkv_ref[:, dq + dkv:dq + 2 * dkv], preferred_element_type=F32).astype(BF16)

    row = lax.broadcasted_iota(jnp.int32, (CHUNK, CHUNK), 0)
    col = lax.broadcasted_iota(jnp.int32, (CHUNK, CHUNK), 1)
    in_cur = col <= row
    no_prev = jnp.where(first_tile, -jnp.inf, 0.0).astype(F32)

    for blk in range(tm // CHUNK):
        r0 = blk * CHUNK
        for kvh in range(N_KV_HEADS):
            d0 = kvh * HEAD_DIM
            kcat = k_scr[r0:r0 + 2 * CHUNK, d0:d0 + HEAD_DIM]
            vcat = v_scr[r0:r0 + 2 * CHUNK, d0:d0 + HEAD_DIM]
            qg = jnp.concatenate(
                [q_scr[r0:r0 + CHUNK, (kvh * KV_GROUP + g) * HEAD_DIM:(kvh * KV_GROUP + g + 1) * HEAD_DIM]
                 for g in range(KV_GROUP)], axis=0)
            s = lax.dot_general(qg, kcat, (((1,), (1,)), ((), ())), preferred_element_type=F32)
            probs = []
            for g in range(KV_GROUP):
                head = kvh * KV_GROUP + g
                s_prev = s[g * CHUNK:(g + 1) * CHUNK, 0:CHUNK]
                s_cur = s[g * CHUNK:(g + 1) * CHUNK, CHUNK:2 * CHUNK]
                if blk == 0:
                    s_prev = s_prev + no_prev
                sm = jnp.where(in_cur, s_cur, s_prev) + bias_ref[head]
                sink = sink_ref[head]
                m = jnp.maximum(jnp.max(sm, axis=-1, keepdims=True), sink)
                p = jnp.exp(sm - m)
                denom = jnp.sum(p, axis=-1, keepdims=True) + jnp.exp(sink - m)
                p = p / denom
                probs.append(jnp.concatenate(
                    [jnp.where(in_cur, 0.0, p), jnp.where(in_cur, p, 0.0)], axis=1).astype(BF16))
            o = jnp.dot(jnp.concatenate(probs, axis=0), vcat, preferred_element_type=F32)
            for g in range(KV_GROUP):
                head = kvh * KV_GROUP + g
                att_scr[r0:r0 + CHUNK, head * HEAD_DIM:(head + 1) * HEAD_DIM] = (
                    o[g * CHUNK:(g + 1) * CHUNK].astype(BF16))

    k_scr[0:CHUNK, :] = k_scr[tm:tm + CHUNK, :]
    v_scr[0:CHUNK, :] = v_scr[tm:tm + CHUNK, :]

    o_ref[...] = h + jnp.dot(att_scr[...], wo_ref[...], preferred_element_type=F32)


def _swa_layer(h, mix_gain, w_qkv, q_gain, k_gain, sinks, w_o, bias_table):
    B, S, D = h.shape
    tm = min(TOKEN_TILE, S)
    dq = N_HEADS * HEAD_DIM
    dkv = N_KV_HEADS * HEAD_DIM
    lane_head = np.arange(LANES) // HEAD_DIM
    seg = jnp.asarray(lane_head[:, None] == lane_head[None, :], BF16)
    return pl.pallas_call(
        _swa_kernel,
        grid=(B, S // tm),
        in_specs=[
            _token_spec(tm, D),
            _const_spec((1, D)),
            _const_spec((D, dq + 2 * dkv)),
            _const_spec((1, dq)),
            _const_spec((1, dkv)),
            pl.BlockSpec(memory_space=pltpu.SMEM),
            _const_spec((N_HEADS, CHUNK, CHUNK)),
            _const_spec((LANES, LANES)),
            _const_spec((dq, D)),
        ],
        out_specs=_token_spec(tm, D),
        out_shape=jax.ShapeDtypeStruct((B, S, D), F32),
        scratch_shapes=[
            pltpu.VMEM((CHUNK + tm, dkv), BF16),
            pltpu.VMEM((CHUNK + tm, dkv), BF16),
            pltpu.VMEM((tm, dq), BF16),
            pltpu.VMEM((tm, dq), BF16),
        ],
        compiler_params=_compiler_params(),
        name="swa_mixer",
    )(h, mix_gain.reshape(1, D), w_qkv.astype(BF16), jnp.tile(q_gain, N_HEADS).reshape(1, dq),
      jnp.tile(k_gain, N_KV_HEADS).reshape(1, dkv), sinks, bias_table, seg, w_o.astype(BF16))


def kernel(x, mix_norm, ffn_norm, sgu_w_in, sgu_v_gain, sgu_w_s, sgu_b_s, sgu_w_out, attn_w_qkv,
           attn_q_gain, attn_k_gain, attn_sinks, attn_w_o, rel_bias, ffn_w_up, ffn_conv_w, ffn_conv_b,
           ffn_w_down):
    depth = mix_norm.shape[0]
    bias_table = _bias_table(rel_bias) if depth > 1 else None
    h = x
    for i in range(depth):
        j = i // 2
        if i % 2 == 0:
            h = _sgu_layer(h, mix_norm[i], sgu_w_in[j], sgu_v_gain[j], sgu_w_s[j], sgu_b_s[j], sgu_w_out[j])
        else:
            h = _swa_layer(h, mix_norm[i], attn_w_qkv[j], attn_q_gain[j], attn_k_gain[j], attn_sinks[j],
                           attn_w_o[j], bias_table)
        h = _ffn_layer(h, ffn_norm[i], ffn_w_up[i], ffn_conv_w[i], ffn_conv_b[i], ffn_w_down[i])
    return h
```

```python
import functools
import math

import numpy as np
import jax
import jax.numpy as jnp
from jax import lax
from jax.experimental import pallas as pl
from jax.experimental.pallas import tpu as pltpu

F32 = jnp.float32
BF16 = jnp.bfloat16

EPS = 1e-6
LANES = 128
SUBLANES = 8
CHUNK = 128
HEAD_DIM = 64
N_HEADS = 16
N_KV_HEADS = 4
KV_GROUP = N_HEADS // N_KV_HEADS
SGU_GROUPS = 16
REL_BUCKETS = 32
REL_MAX_DIST = 128
CONV_WIDTH = 3

TOKEN_TILE = 512
SGU_COL_TILE = 512
FFN_COL_TILE = 256
QK_COL_TILE = 256
VMEM_LIMIT_BYTES = 56 * 1024 * 1024


def _rms_norm(x, gain):
    ms = jnp.mean(x * x, axis=-1, keepdims=True)
    return x * lax.rsqrt(ms + EPS) * gain


def _const_spec(shape):
    zeros = (0,) * len(shape)
    return pl.BlockSpec(shape, lambda b, j: zeros, pipeline_mode=pl.Buffered(1))


def _token_spec(tm, d):
    return pl.BlockSpec((None, tm, d), lambda b, j: (b, j, 0))


def _compiler_params():
    return pltpu.CompilerParams(
        dimension_semantics=("arbitrary", "arbitrary"),
        vmem_limit_bytes=VMEM_LIMIT_BYTES,
    )


def _sgu_kernel(h_ref, mg_ref, win_ref, vg_ref, ws_ref, bs_ref, wout_ref, o_ref,
                u_scr, v_scr, us_scr, wsm_scr):
    tm = h_ref.shape[0]
    width = u_scr.shape[1]

    @pl.when((pl.program_id(0) == 0) & (pl.program_id(1) == 0))
    def _():
        row = lax.broadcasted_iota(jnp.int32, (CHUNK, CHUNK), 0)
        col = lax.broadcasted_iota(jnp.int32, (CHUNK, CHUNK), 1)
        for g in range(SGU_GROUPS):
            wsm_scr[g] = jnp.where(row >= col, ws_ref[g], 0.0).astype(BF16)

    h = h_ref[...]
    hn = _rms_norm(h, mg_ref[...]).astype(BF16)

    ssq = jnp.zeros((tm, 1), F32)
    n_col = (2 * width) // SGU_COL_TILE
    for c in range(n_col):
        c0 = c * SGU_COL_TILE
        z = jnp.dot(hn, win_ref[:, c0:c0 + SGU_COL_TILE], preferred_element_type=F32)
        z = 0.5 * z * (1.0 + lax.erf(z * np.float32(math.sqrt(0.5))))
        if c0 < width:
            u_scr[:, c0:c0 + SGU_COL_TILE] = z
        else:
            ssq = ssq + jnp.sum(z * z, axis=-1, keepdims=True)
            v_scr[:, c0 - width:c0 - width + SGU_COL_TILE] = z
    rstd = lax.rsqrt(ssq * np.float32(1.0 / width) + EPS)

    n_chunk = tm // CHUNK
    for g in range(SGU_GROUPS):
        g0 = g * LANES
        vn = jnp.concatenate(
            [(v_scr[cb * CHUNK:(cb + 1) * CHUNK, g0:g0 + LANES] * rstd[cb * CHUNK:(cb + 1) * CHUNK]
              * vg_ref[:, g0:g0 + LANES]).astype(BF16) for cb in range(n_chunk)], axis=1)
        s = jnp.dot(wsm_scr[g], vn, preferred_element_type=F32) + bs_ref[:, g:g + 1]
        for cb in range(n_chunk):
            r0 = cb * CHUNK
            us_scr[r0:r0 + CHUNK, g0:g0 + LANES] = (
                u_scr[r0:r0 + CHUNK, g0:g0 + LANES] * s[:, cb * CHUNK:(cb + 1) * CHUNK]).astype(BF16)

    o_ref[...] = h + jnp.dot(us_scr[...], wout_ref[...], preferred_element_type=F32)


def _sgu_layer(h, mix_gain, w_in, v_gain, w_s, b_s, w_out):
    B, S, D = h.shape
    tm = min(TOKEN_TILE, S)
    width = w_out.shape[0]
    return pl.pallas_call(
        _sgu_kernel,
        grid=(B, S // tm),
        in_specs=[
            _token_spec(tm, D),
            _const_spec((1, D)),
            _const_spec((D, 2 * width)),
            _const_spec((1, width)),
            _const_spec((SGU_GROUPS, CHUNK, CHUNK)),
            _const_spec((CHUNK, SGU_GROUPS)),
            _const_spec((width, D)),
        ],
        out_specs=_token_spec(tm, D),
        out_shape=jax.ShapeDtypeStruct((B, S, D), F32),
        scratch_shapes=[
            pltpu.VMEM((tm, width), F32),
            pltpu.VMEM((tm, width), F32),
            pltpu.VMEM((tm, width), BF16),
            pltpu.VMEM((SGU_GROUPS, CHUNK, CHUNK), BF16),
        ],
        compiler_params=_compiler_params(),
        name="sgu_mixer",
    )(h, mix_gain.reshape(1, D), w_in.astype(BF16), v_gain.reshape(1, width), w_s,
      b_s.T, w_out.astype(BF16))


def _ffn_kernel(h_ref, ng_ref, wup_ref, cw_ref, cb_ref, wdown_ref, o_ref,
                carry_scr, act_scr, perm_scr):
    tm, d = h_ref.shape
    seg = tm // SUBLANES
    pitch = perm_scr.shape[1] // SUBLANES
    d_ff = act_scr.shape[1]
    ct = FFN_COL_TILE

    @pl.when(pl.program_id(1) == 0)
    def _():
        carry_scr[...] = jnp.zeros_like(carry_scr)

    for r in range(seg):
        for c in range(d // LANES):
            perm_scr[c, pl.ds(r, SUBLANES, stride=pitch), :] = (
                h_ref[r * SUBLANES:(r + 1) * SUBLANES, c * LANES:(c + 1) * LANES])
    h = jnp.concatenate(
        [jnp.concatenate([perm_scr[c, p * pitch:p * pitch + seg, :] for c in range(d // LANES)], axis=1)
         for p in range(SUBLANES)], axis=0)
    hn = _rms_norm(h, ng_ref[...]).astype(BF16)
    first_row = lax.broadcasted_iota(jnp.int32, (seg, 2 * ct), 0) == 0

    for c in range(d_ff // ct):
        c0 = 2 * ct * c
        a = jnp.dot(hn, wup_ref[:, c0:c0 + 2 * ct], preferred_element_type=F32)
        wrap1 = jnp.where(first_row, carry_scr[1:2, c0:c0 + 2 * ct], pltpu.roll(a[7 * seg:8 * seg], 1, 0))
        wrap2 = jnp.where(first_row, carry_scr[0:1, c0:c0 + 2 * ct], pltpu.roll(a[6 * seg:7 * seg], 1, 0))
        carry_scr[0:1, c0:c0 + 2 * ct] = a[7 * seg - 1:7 * seg]
        carry_scr[1:2, c0:c0 + 2 * ct] = a[8 * seg - 1:8 * seg]
        a1 = jnp.concatenate([wrap1, a[0:7 * seg]], axis=0)
        a2 = jnp.concatenate([wrap2, wrap1, a[0:6 * seg]], axis=0)
        conv = (cw_ref[0:1, c0:c0 + 2 * ct] * a2 + cw_ref[1:2, c0:c0 + 2 * ct] * a1
                + cw_ref[2:3, c0:c0 + 2 * ct] * a + cb_ref[:, c0:c0 + 2 * ct])
        gate = conv[:, 0:ct]
        val = conv[:, ct:2 * ct]
        act_scr[:, c * ct:(c + 1) * ct] = (gate / (1.0 + jnp.exp(-gate)) * val).astype(BF16)

    out = h + jnp.dot(act_scr[...], wdown_ref[...], preferred_element_type=F32)

    for p in range(SUBLANES):
        for c in range(d // LANES):
            perm_scr[c, p * pitch:p * pitch + seg, :] = out[p * seg:(p + 1) * seg, c * LANES:(c + 1) * LANES]
    for r in range(seg):
        for c in range(d // LANES):
            o_ref[r * SUBLANES:(r + 1) * SUBLANES, c * LANES:(c + 1) * LANES] = (
                perm_scr[c, pl.ds(r, SUBLANES, stride=pitch), :])


def _pair_gate_value_cols(w, d_ff):
    lead = w.shape[:-1]
    w = w.reshape(lead + (2, d_ff // FFN_COL_TILE, FFN_COL_TILE))
    return jnp.swapaxes(w, -3, -2).reshape(lead + (2 * d_ff,))


def _ffn_layer(h, norm_gain, w_up, conv_w, conv_b, w_down):
    B, S, D = h.shape
    tm = min(TOKEN_TILE, S)
    d_ff = w_down.shape[0]
    return pl.pallas_call(
        _ffn_kernel,
        grid=(B, S // tm),
        in_specs=[
            _token_spec(tm, D),
            _const_spec((1, D)),
            _const_spec((D, 2 * d_ff)),
            _const_spec((CONV_WIDTH, 2 * d_ff)),
            _const_spec((1, 2 * d_ff)),
            _const_spec((d_ff, D)),
        ],
        out_specs=_token_spec(tm, D),
        out_shape=jax.ShapeDtypeStruct((B, S, D), F32),
        scratch_shapes=[
            pltpu.VMEM((SUBLANES, 2 * d_ff), F32),
            pltpu.VMEM((tm, d_ff), BF16),
            pltpu.VMEM((D // LANES, tm + SUBLANES * SUBLANES, LANES), F32),
        ],
        compiler_params=_compiler_params(),
        name="conv_glu_ffn",
    )(h, norm_gain.reshape(1, D), _pair_gate_value_cols(w_up.astype(BF16), d_ff),
      _pair_gate_value_cols(conv_w, d_ff), _pair_gate_value_cols(conv_b.reshape(1, 2 * d_ff), d_ff),
      w_down.astype(BF16))


def _merged_rel_buckets():
    i = np.arange(CHUNK)[:, None]
    j = np.arange(CHUNK)[None, :]
    n = np.where(j <= i, i - j, CHUNK + i - j)
    max_exact = REL_BUCKETS // 2
    large = max_exact + (np.log(np.maximum(n, 1).astype(np.float32) / max_exact)
                         / math.log(REL_MAX_DIST / max_exact)
                         * (REL_BUCKETS - max_exact)).astype(np.int32)
    large = np.minimum(large, REL_BUCKETS - 1)
    return np.where(n < max_exact, n, large).astype(np.int32)


def _bias_kernel(bucket_ref, rb_ref, o_ref):
    head = pl.program_id(0)
    bucket = bucket_ref[...]
    acc = jnp.zeros(bucket.shape, F32)
    for b in range(REL_BUCKETS):
        acc = jnp.where(bucket == b, rb_ref[b, head], acc)
    o_ref[...] = acc


def _bias_table(rel_bias):
    bucket = jnp.asarray(_merged_rel_buckets())
    return pl.pallas_call(
        _bias_kernel,
        grid=(N_HEADS,),
        in_specs=[
            pl.BlockSpec((CHUNK, CHUNK), lambda h: (0, 0)),
            pl.BlockSpec(memory_space=pltpu.SMEM),
        ],
        out_specs=pl.BlockSpec((None, CHUNK, CHUNK), lambda h: (h, 0, 0)),
        out_shape=jax.ShapeDtypeStruct((N_HEADS, CHUNK, CHUNK), F32),
        name="rel_bias_table",
    )(bucket, rel_bias)


def _head_sumsq(x, seg):
    sq = x * x
    hi = sq.astype(BF16)
    lo = (sq - hi.astype(F32)).astype(BF16)
    return (jnp.dot(hi, seg, preferred_element_type=F32)
            + jnp.dot(lo, seg, preferred_element_type=F32))


def _swa_kernel(h_ref, mg_ref, wqkv_ref, qg_ref, kg_ref, sink_ref, bias_ref, seg_ref, wo_ref, o_ref,
                k_scr, v_scr, q_scr, att_scr):
    tm = h_ref.shape[0]
    dq = N_HEADS * HEAD_DIM
    dkv = N_KV_HEADS * HEAD_DIM
    first_tile = pl.program_id(1) == 0

    @pl.when(first_tile)
    def _():
        k_scr[0:CHUNK, :] = jnp.zeros((CHUNK, dkv), BF16)
        v_scr[0:CHUNK, :] = jnp.zeros((CHUNK, dkv), BF16)

    h = h_ref[...]
    hn = _rms_norm(h, mg_ref[...]).astype(BF16)
    seg = seg_ref[...]
    inv_hd = np.float32(1.0 / HEAD_DIM)

    def project(w0):
        return jnp.dot(hn, wqkv_ref[:, w0:w0 + 2 * QK_COL_TILE], preferred_element_type=F32)

    xw_next = project(0)
    for w0 in range(0, dq + 2 * dkv, 2 * QK_COL_TILE):
        xw = xw_next
        if w0 + 2 * QK_COL_TILE < dq + 2 * dkv:
            xw_next = project(w0 + 2 * QK_COL_TILE)
        for c0 in range(w0, w0 + 2 * QK_COL_TILE, QK_COL_TILE):
            x = xw[:, c0 - w0:c0 - w0 + QK_COL_TILE]
            if c0 >= dq + dkv:
                v_scr[CHUNK:CHUNK + tm, c0 - dq - dkv:c0 - dq - dkv + QK_COL_TILE] = x.astype(BF16)
                continue
            r = lax.rsqrt(_head_sumsq(x, seg) * inv_hd + EPS)
            if c0 < dq:
                gain = qg_ref[:, c0:c0 + QK_COL_TILE] * np.float32(HEAD_DIM ** -0.5)
                q_scr[:, c0:c0 + QK_COL_TILE] = (x * r * gain).astype(BF16)
            else:
                k_scr[CHUNK:CHUNK + tm, c0 - dq:c0 - dq + QK_COL_TILE] = (
                    x * r * kg_ref[:, c0 - dq:c0 - dq + QK_COL_TILE]).astype(BF16)

    row = lax.broadcasted_iota(jnp.int32, (CHUNK, CHUNK), 0)
    col = lax.broadcasted_iota(jnp.int32, (CHUNK, CHUNK), 1)
    in_cur = col <= row
    no_prev = jnp.where(first_tile, -jnp.inf, 0.0).astype(F32)

    def scores(blk, kvh):
        r0 = blk * CHUNK
        kcat = k_scr[r0:r0 + 2 * CHUNK, kvh * HEAD_DIM:(kvh + 1) * HEAD_DIM]
        qg = jnp.concatenate(
            [q_scr[r0:r0 + CHUNK, (kvh * KV_GROUP + g) * HEAD_DIM:(kvh * KV_GROUP + g + 1) * HEAD_DIM]
             for g in range(KV_GROUP)], axis=0)
        return lax.dot_general(qg, kcat, (((1,), (1,)), ((), ())), preferred_element_type=F32)

    def attend(blk, kvh, s):
        r0 = blk * CHUNK
        vcat = v_scr[r0:r0 + 2 * CHUNK, kvh * HEAD_DIM:(kvh + 1) * HEAD_DIM]
        probs = []
        for g in range(KV_GROUP):
            head = kvh * KV_GROUP + g
            s_prev = s[g * CHUNK:(g + 1) * CHUNK, 0:CHUNK]
            s_cur = s[g * CHUNK:(g + 1) * CHUNK, CHUNK:2 * CHUNK]
            if blk == 0:
                s_prev = s_prev + no_prev
            sm = jnp.where(in_cur, s_cur, s_prev) + bias_ref[head]
            sink = sink_ref[head]
            m = jnp.maximum(jnp.max(sm, axis=-1, keepdims=True), sink)
            p = jnp.exp(sm - m)
            denom = jnp.sum(p, axis=-1, keepdims=True) + jnp.exp(sink - m)
            p = p / denom
            probs.append(jnp.concatenate(
                [jnp.where(in_cur, 0.0, p), jnp.where(in_cur, p, 0.0)], axis=1).astype(BF16))
        o = jnp.dot(jnp.concatenate(probs, axis=0), vcat, preferred_element_type=F32)
        for g in range(KV_GROUP):
            head = kvh * KV_GROUP + g
            att_scr[r0:r0 + CHUNK, head * HEAD_DIM:(head + 1) * HEAD_DIM] = (
                o[g * CHUNK:(g + 1) * CHUNK].astype(BF16))

    tasks = [(blk, kvh) for blk in range(tm // CHUNK) for kvh in range(N_KV_HEADS)]
    s_next = scores(*tasks[0])
    for t, task in enumerate(tasks):
        s_this = s_next
        if t + 1 < len(tasks):
            s_next = scores(*tasks[t + 1])
        attend(*task, s_this)

    k_scr[0:CHUNK, :] = k_scr[tm:tm + CHUNK, :]
    v_scr[0:CHUNK, :] = v_scr[tm:tm + CHUNK, :]

    o_ref[...] = h + jnp.dot(att_scr[...], wo_ref[...], preferred_element_type=F32)


def _swa_layer(h, mix_gain, w_qkv, q_gain, k_gain, sinks, w_o, bias_table):
    B, S, D = h.shape
    tm = min(TOKEN_TILE, S)
    dq = N_HEADS * HEAD_DIM
    dkv = N_KV_HEADS * HEAD_DIM
    lane_head = np.arange(QK_COL_TILE) // HEAD_DIM
    seg = jnp.asarray(lane_head[:, None] == lane_head[None, :], BF16)
    return pl.pallas_call(
        _swa_kernel,
        grid=(B, S // tm),
        in_specs=[
            _token_spec(tm, D),
            _const_spec((1, D)),
            _const_spec((D, dq + 2 * dkv)),
            _const_spec((1, dq)),
            _const_spec((1, dkv)),
            pl.BlockSpec(memory_space=pltpu.SMEM),
            _const_spec((N_HEADS, CHUNK, CHUNK)),
            _const_spec((QK_COL_TILE, QK_COL_TILE)),
            _const_spec((dq, D)),
        ],
        out_specs=_token_spec(tm, D),
        out_shape=jax.ShapeDtypeStruct((B, S, D), F32),
        scratch_shapes=[
            pltpu.VMEM((CHUNK + tm, dkv), BF16),
            pltpu.VMEM((CHUNK + tm, dkv), BF16),
            pltpu.VMEM((tm, dq), BF16),
            pltpu.VMEM((tm, dq), BF16),
        ],
        compiler_params=_compiler_params(),
        name="swa_mixer",
    )(h, mix_gain.reshape(1, D), w_qkv.astype(BF16), jnp.tile(q_gain, N_HEADS).reshape(1, dq),
      jnp.tile(k_gain, N_KV_HEADS).reshape(1, dkv), sinks, bias_table, seg, w_o.astype(BF16))


def kernel(x, mix_norm, ffn_norm, sgu_w_in, sgu_v_gain, sgu_w_s, sgu_b_s, sgu_w_out, attn_w_qkv,
           attn_q_gain, attn_k_gain, attn_sinks, attn_w_o, rel_bias, ffn_w_up, ffn_conv_w, ffn_conv_b,
           ffn_w_down):
    depth = mix_norm.shape[0]
    bias_table = _bias_table(rel_bias) if depth > 1 else None
    h = x
    for i in range(depth):
        j = i // 2
        if i % 2 == 0:
            h = _sgu_layer(h, mix_norm[i], sgu_w_in[j], sgu_v_gain[j], sgu_w_s[j], sgu_b_s[j], sgu_w_out[j])
        else:
            h = _swa_layer(h, mix_norm[i], attn_w_qkv[j], attn_q_gain[j], attn_k_gain[j], attn_sinks[j],
                           attn_w_o[j], bias_table)
        h = _ffn_layer(h, ffn_norm[i], ffn_w_up[i], ffn_conv_w[i], ffn_conv_b[i], ffn_w_down[i])
    return h
```

```python
import functools
import math

import numpy as np
import jax
import jax.numpy as jnp
from jax import lax
from jax.experimental import pallas as pl
from jax.experimental.pallas import tpu as pltpu

F32 = jnp.float32
BF16 = jnp.bfloat16

EPS = 1e-6
LANES = 128
SUBLANES = 8
CHUNK = 128
HEAD_DIM = 64
N_HEADS = 16
N_KV_HEADS = 4
KV_GROUP = N_HEADS // N_KV_HEADS
SGU_GROUPS = 16
REL_BUCKETS = 32
REL_MAX_DIST = 128
CONV_WIDTH = 3

TOKEN_TILE = 512
SGU_COL_TILE = 512
FFN_COL_TILE = 256
QK_COL_TILE = 256
CHAINS_IN_FLIGHT = 4
VMEM_LIMIT_BYTES = 56 * 1024 * 1024


def _rms_norm(x, gain):
    ms = jnp.mean(x * x, axis=-1, keepdims=True)
    return x * lax.rsqrt(ms + EPS) * gain


def _interleave(chains, in_flight):
    waiting = list(chains)
    running, finished = [], set()
    while waiting or running:
        while waiting and len(running) < in_flight and all(id(p) in finished for p in waiting[0][1]):
            running.append(waiting.pop(0)[0])
        for chain in list(running):
            if next(chain, "done") == "done":
                running.remove(chain)
                finished.add(id(chain))


def _const_spec(shape):
    zeros = (0,) * len(shape)
    return pl.BlockSpec(shape, lambda b, j: zeros, pipeline_mode=pl.Buffered(1))


def _token_spec(tm, d):
    return pl.BlockSpec((None, tm, d), lambda b, j: (b, j, 0))


def _compiler_params():
    return pltpu.CompilerParams(
        dimension_semantics=("arbitrary", "arbitrary"),
        vmem_limit_bytes=VMEM_LIMIT_BYTES,
    )


def _sgu_kernel(h_ref, mg_ref, win_ref, vg_ref, ws_ref, bs_ref, wout_ref, o_ref,
                u_scr, v_scr, us_scr, wsm_scr):
    tm = h_ref.shape[0]
    width = u_scr.shape[1]

    @pl.when((pl.program_id(0) == 0) & (pl.program_id(1) == 0))
    def _():
        row = lax.broadcasted_iota(jnp.int32, (CHUNK, CHUNK), 0)
        col = lax.broadcasted_iota(jnp.int32, (CHUNK, CHUNK), 1)
        for g in range(SGU_GROUPS):
            wsm_scr[g] = jnp.where(row >= col, ws_ref[g], 0.0).astype(BF16)

    h = h_ref[...]
    hn = _rms_norm(h, mg_ref[...]).astype(BF16)

    v_sumsq = []

    def in_proj(c0):
        z = jnp.dot(hn, win_ref[:, c0:c0 + SGU_COL_TILE], preferred_element_type=F32)
        yield
        z = 0.5 * z * (1.0 + lax.erf(z * np.float32(math.sqrt(0.5))))
        if c0 < width:
            u_scr[:, c0:c0 + SGU_COL_TILE] = z
        else:
            v_sumsq.append(jnp.sum(z * z, axis=-1, keepdims=True))
            v_scr[:, c0 - width:c0 - width + SGU_COL_TILE] = z

    _interleave([(in_proj(c0), ()) for c0 in range(width, 2 * width, SGU_COL_TILE)], CHAINS_IN_FLIGHT)
    rstd = lax.rsqrt(sum(v_sumsq) * np.float32(1.0 / width) + EPS)

    n_chunk = tm // CHUNK

    def gate(g):
        g0 = g * LANES
        vn = jnp.concatenate(
            [(v_scr[cb * CHUNK:(cb + 1) * CHUNK, g0:g0 + LANES] * rstd[cb * CHUNK:(cb + 1) * CHUNK]
              * vg_ref[:, g0:g0 + LANES]).astype(BF16) for cb in range(n_chunk)], axis=1)
        yield
        s = jnp.dot(wsm_scr[g], vn, preferred_element_type=F32) + bs_ref[:, g:g + 1]
        yield
        for cb in range(n_chunk):
            r0 = cb * CHUNK
            us_scr[r0:r0 + CHUNK, g0:g0 + LANES] = (
                u_scr[r0:r0 + CHUNK, g0:g0 + LANES] * s[:, cb * CHUNK:(cb + 1) * CHUNK]).astype(BF16)

    u_projs = [in_proj(c0) for c0 in range(0, width, SGU_COL_TILE)]
    groups_per_tile = SGU_COL_TILE // LANES
    chains = [(u_projs[0], ())]
    for i, u_proj in enumerate(u_projs):
        if i + 1 < len(u_projs):
            chains.append((u_projs[i + 1], ()))
        chains.extend((gate(g), (u_proj,)) for g in range(i * groups_per_tile, (i + 1) * groups_per_tile))
    _interleave(chains, CHAINS_IN_FLIGHT)

    o_ref[...] = h + jnp.dot(us_scr[...], wout_ref[...], preferred_element_type=F32)


def _sgu_layer(h, mix_gain, w_in, v_gain, w_s, b_s, w_out):
    B, S, D = h.shape
    tm = min(TOKEN_TILE, S)
    width = w_out.shape[0]
    return pl.pallas_call(
        _sgu_kernel,
        grid=(B, S // tm),
        in_specs=[
            _token_spec(tm, D),
            _const_spec((1, D)),
            _const_spec((D, 2 * width)),
            _const_spec((1, width)),
            _const_spec((SGU_GROUPS, CHUNK, CHUNK)),
            _const_spec((CHUNK, SGU_GROUPS)),
            _const_spec((width, D)),
        ],
        out_specs=_token_spec(tm, D),
        out_shape=jax.ShapeDtypeStruct((B, S, D), F32),
        scratch_shapes=[
            pltpu.VMEM((tm, width), F32),
            pltpu.VMEM((tm, width), F32),
            pltpu.VMEM((tm, width), BF16),
            pltpu.VMEM((SGU_GROUPS, CHUNK, CHUNK), BF16),
        ],
        compiler_params=_compiler_params(),
        name="sgu_mixer",
    )(h, mix_gain.reshape(1, D), w_in.astype(BF16), v_gain.reshape(1, width), w_s,
      b_s.T, w_out.astype(BF16))


def _ffn_kernel(h_ref, ng_ref, wup_ref, cw_ref, cb_ref, wdown_ref, o_ref,
                carry_scr, act_scr, perm_scr):
    tm, d = h_ref.shape
    seg = tm // SUBLANES
    pitch = perm_scr.shape[1] // SUBLANES
    d_ff = act_scr.shape[1]
    ct = FFN_COL_TILE

    @pl.when(pl.program_id(1) == 0)
    def _():
        carry_scr[...] = jnp.zeros_like(carry_scr)

    for r in range(seg):
        for c in range(d // LANES):
            perm_scr[c, pl.ds(r, SUBLANES, stride=pitch), :] = (
                h_ref[r * SUBLANES:(r + 1) * SUBLANES, c * LANES:(c + 1) * LANES])
    h = jnp.concatenate(
        [jnp.concatenate([perm_scr[c, p * pitch:p * pitch + seg, :] for c in range(d // LANES)], axis=1)
         for p in range(SUBLANES)], axis=0)
    hn = _rms_norm(h, ng_ref[...]).astype(BF16)
    first_row = lax.broadcasted_iota(jnp.int32, (seg, 2 * ct), 0) == 0

    def pair(ref, rows, c):
        return jnp.concatenate([ref[rows, c * ct:(c + 1) * ct], ref[rows, d_ff + c * ct:d_ff + (c + 1) * ct]],
                               axis=1)

    def up_tile(c):
        c0 = 2 * ct * c
        a = jnp.dot(hn, pair(wup_ref, slice(None), c), preferred_element_type=F32)
        yield
        wrap1 = jnp.where(first_row, carry_scr[1:2, c0:c0 + 2 * ct], pltpu.roll(a[7 * seg:8 * seg], 1, 0))
        wrap2 = jnp.where(first_row, carry_scr[0:1, c0:c0 + 2 * ct], pltpu.roll(a[6 * seg:7 * seg], 1, 0))
        carry_scr[0:1, c0:c0 + 2 * ct] = a[7 * seg - 1:7 * seg]
        carry_scr[1:2, c0:c0 + 2 * ct] = a[8 * seg - 1:8 * seg]
        a1 = jnp.concatenate([wrap1, a[0:7 * seg]], axis=0)
        a2 = jnp.concatenate([wrap2, wrap1, a[0:6 * seg]], axis=0)
        conv = (pair(cw_ref, slice(0, 1), c) * a2 + pair(cw_ref, slice(1, 2), c) * a1
                + pair(cw_ref, slice(2, 3), c) * a + pair(cb_ref, slice(None), c))
        yield
        gate = conv[:, 0:ct]
        val = conv[:, ct:2 * ct]
        act_scr[:, c * ct:(c + 1) * ct] = (gate / (1.0 + jnp.exp(-gate)) * val).astype(BF16)

    _interleave([(up_tile(c), ()) for c in range(d_ff // ct)], CHAINS_IN_FLIGHT)

    out = h + jnp.dot(act_scr[...], wdown_ref[...], preferred_element_type=F32)

    for p in range(SUBLANES):
        for c in range(d // LANES):
            perm_scr[c, p * pitch:p * pitch + seg, :] = out[p * seg:(p + 1) * seg, c * LANES:(c + 1) * LANES]
    for r in range(seg):
        for c in range(d // LANES):
            o_ref[r * SUBLANES:(r + 1) * SUBLANES, c * LANES:(c + 1) * LANES] = (
                perm_scr[c, pl.ds(r, SUBLANES, stride=pitch), :])


def _ffn_layer(h, norm_gain, w_up, conv_w, conv_b, w_down):
    B, S, D = h.shape
    tm = min(TOKEN_TILE, S)
    d_ff = w_down.shape[0]
    return pl.pallas_call(
        _ffn_kernel,
        grid=(B, S // tm),
        in_specs=[
            _token_spec(tm, D),
            _const_spec((1, D)),
            _const_spec((D, 2 * d_ff)),
            _const_spec((CONV_WIDTH, 2 * d_ff)),
            _const_spec((1, 2 * d_ff)),
            _const_spec((d_ff, D)),
        ],
        out_specs=_token_spec(tm, D),
        out_shape=jax.ShapeDtypeStruct((B, S, D), F32),
        scratch_shapes=[
            pltpu.VMEM((SUBLANES, 2 * d_ff), F32),
            pltpu.VMEM((tm, d_ff), BF16),
            pltpu.VMEM((D // LANES, tm + SUBLANES * SUBLANES, LANES), F32),
        ],
        compiler_params=_compiler_params(),
        name="conv_glu_ffn",
    )(h, norm_gain.reshape(1, D), w_up.astype(BF16), conv_w, conv_b.reshape(1, 2 * d_ff),
      w_down.astype(BF16))


def _band_rel_buckets():
    dist = (np.arange(CHUNK)[:, None] + CHUNK) - np.arange(2 * CHUNK)[None, :]
    n = np.maximum(dist, 0)
    max_exact = REL_BUCKETS // 2
    large = max_exact + (np.log(np.maximum(n, 1).astype(np.float32) / max_exact)
                         / math.log(REL_MAX_DIST / max_exact)
                         * (REL_BUCKETS - max_exact)).astype(np.int32)
    large = np.minimum(large, REL_BUCKETS - 1)
    bucket = np.where(n < max_exact, n, large)
    return np.where((dist >= 0) & (dist < CHUNK), bucket, -1).astype(np.int32)


def _bias_kernel(bucket_ref, rb_ref, o_ref):
    head = pl.program_id(0)
    bucket = bucket_ref[...]
    acc = jnp.full(bucket.shape, -jnp.inf, F32)
    for b in range(REL_BUCKETS):
        acc = jnp.where(bucket == b, rb_ref[b, head], acc)
    o_ref[...] = acc.T


def _bias_table(rel_bias):
    bucket = jnp.asarray(_band_rel_buckets())
    return pl.pallas_call(
        _bias_kernel,
        grid=(N_HEADS,),
        in_specs=[
            pl.BlockSpec((CHUNK, 2 * CHUNK), lambda h: (0, 0)),
            pl.BlockSpec(memory_space=pltpu.SMEM),
        ],
        out_specs=pl.BlockSpec((None, 2 * CHUNK, CHUNK),
                               lambda h: (h // KV_GROUP, h % 2, (h % KV_GROUP) // 2)),
        out_shape=jax.ShapeDtypeStruct((N_KV_HEADS, 4 * CHUNK, 2 * CHUNK), F32),
        name="rel_bias_table",
    )(bucket, rel_bias)


def _head_sumsq(x, seg):
    sq = x * x
    hi = sq.astype(BF16)
    lo = (sq - hi.astype(F32)).astype(BF16)
    return (jnp.dot(hi, seg, preferred_element_type=F32)
            + jnp.dot(lo, seg, preferred_element_type=F32))


def _swa_kernel(h_ref, mg_ref, wqkv_ref, qg_ref, kg_ref, sink_ref, bias_ref, seg_ref, wo_ref, o_ref,
                klo_scr, khi_scr, vt_scr, q_scr, att_scr):
    tm = h_ref.shape[0]
    dq = N_HEADS * HEAD_DIM
    dkv = N_KV_HEADS * HEAD_DIM
    first_tile = pl.program_id(1) == 0
    vt_rows = 3 * HEAD_DIM

    @pl.when(first_tile)
    def _():
        klo_scr[0:CHUNK, :] = jnp.zeros((CHUNK, klo_scr.shape[1]), BF16)
        khi_scr[0:CHUNK, :] = jnp.zeros((CHUNK, khi_scr.shape[1]), BF16)
        vt_scr[...] = jnp.zeros(vt_scr.shape, BF16)

    def store_lo_hi(lo_scr, hi_scr, x):
        zeros = jnp.zeros((tm, HEAD_DIM), F32)
        for kvh in range(N_KV_HEADS):
            x_h = x[:, kvh * HEAD_DIM:(kvh + 1) * HEAD_DIM]
            lo_scr[CHUNK:CHUNK + tm, kvh * LANES:(kvh + 1) * LANES] = (
                jnp.concatenate([x_h, zeros], axis=1).astype(BF16))
            hi_scr[CHUNK:CHUNK + tm, kvh * LANES:(kvh + 1) * LANES] = (
                jnp.concatenate([zeros, x_h], axis=1).astype(BF16))

    def store_transposed(x):
        xt = x.T
        for kvh in range(N_KV_HEADS):
            vt_scr[kvh * vt_rows + HEAD_DIM:kvh * vt_rows + 2 * HEAD_DIM, CHUNK:CHUNK + tm] = (
                xt[kvh * HEAD_DIM:(kvh + 1) * HEAD_DIM].astype(BF16))

    h = h_ref[...]
    hn = _rms_norm(h, mg_ref[...]).astype(BF16)
    seg = seg_ref[...]
    inv_hd = np.float32(1.0 / HEAD_DIM)

    def project(w0):
        xw = jnp.dot(hn, wqkv_ref[:, w0:w0 + 2 * QK_COL_TILE], preferred_element_type=F32)
        yield
        for c0 in range(w0, w0 + 2 * QK_COL_TILE, QK_COL_TILE):
            x = xw[:, c0 - w0:c0 - w0 + QK_COL_TILE]
            if c0 >= dq + dkv:
                store_transposed(x)
                yield
                continue
            r = lax.rsqrt(_head_sumsq(x, seg) * inv_hd + EPS)
            yield
            if c0 < dq:
                gain = qg_ref[:, c0:c0 + QK_COL_TILE] * np.float32(HEAD_DIM ** -0.5)
                q_scr[:, c0:c0 + QK_COL_TILE] = (x * r * gain).astype(BF16)
            else:
                store_lo_hi(klo_scr, khi_scr, x * r * kg_ref[:, c0 - dq:c0 - dq + QK_COL_TILE])
            yield

    band_row = lax.broadcasted_iota(jnp.int32, (4 * CHUNK, 1), 0) % (2 * CHUNK)
    no_prev = jnp.where(first_tile & (band_row < CHUNK), -jnp.inf, 0.0).astype(F32)
    upper_pair = lax.broadcasted_iota(jnp.int32, (1, 2 * CHUNK), 1) >= CHUNK

    def scores(blk, kvh):
        r0 = blk * CHUNK
        c0 = kvh * KV_GROUP * HEAD_DIM
        q2 = jnp.concatenate([q_scr[r0:r0 + CHUNK, c0:c0 + LANES],
                              q_scr[r0:r0 + CHUNK, c0 + LANES:c0 + 2 * LANES]], axis=0)
        k2 = jnp.concatenate([klo_scr[r0:r0 + 2 * CHUNK, kvh * LANES:(kvh + 1) * LANES],
                              khi_scr[r0:r0 + 2 * CHUNK, kvh * LANES:(kvh + 1) * LANES]], axis=0)
        return lax.dot_general(k2, q2, (((1,), (1,)), ((), ())), preferred_element_type=F32)

    def attend(blk, kvh):
        r0 = blk * CHUNK
        c0 = kvh * KV_GROUP * HEAD_DIM
        s = scores(blk, kvh)
        yield
        s = s + bias_ref[kvh]
        if blk == 0:
            s = s + no_prev
        probs = []
        for odd in range(2):
            t = s[odd * 2 * CHUNK:(odd + 1) * 2 * CHUNK]
            sink = jnp.where(upper_pair, sink_ref[kvh * KV_GROUP + 2 + odd], sink_ref[kvh * KV_GROUP + odd])
            m = jnp.maximum(jnp.max(t, axis=0, keepdims=True), sink)
            yield
            p = jnp.exp(t - m)
            denom = jnp.sum(p, axis=0, keepdims=True) + jnp.exp(sink - m)
            yield
            probs.append((p / denom).astype(BF16))
            yield
        v0 = kvh * vt_rows
        vt2 = jnp.concatenate([vt_scr[v0 + HEAD_DIM:v0 + 3 * HEAD_DIM, r0:r0 + 2 * CHUNK],
                               vt_scr[v0:v0 + 2 * HEAD_DIM, r0:r0 + 2 * CHUNK]], axis=1)
        ot = jnp.dot(vt2, jnp.concatenate(probs, axis=0), preferred_element_type=F32)
        yield
        att_scr[r0:r0 + CHUNK, c0:c0 + LANES] = ot[:, 0:CHUNK].T.astype(BF16)
        att_scr[r0:r0 + CHUNK, c0 + LANES:c0 + 2 * LANES] = ot[:, CHUNK:2 * CHUNK].T.astype(BF16)

    kv_proj = project(dq)
    q_projs = [project(w0) for w0 in range(0, dq, 2 * QK_COL_TILE)]
    chains = [(kv_proj, ())] + [(q_proj, ()) for q_proj in q_projs]
    kvh_per_proj = 2 * QK_COL_TILE // (KV_GROUP * HEAD_DIM)
    for kvh in range(N_KV_HEADS):
        chains.extend((attend(blk, kvh), (kv_proj, q_projs[kvh // kvh_per_proj])) for blk in range(tm // CHUNK))
    _interleave(chains, CHAINS_IN_FLIGHT)

    klo_scr[0:CHUNK, :] = klo_scr[tm:tm + CHUNK, :]
    khi_scr[0:CHUNK, :] = khi_scr[tm:tm + CHUNK, :]
    vt_scr[:, 0:CHUNK] = vt_scr[:, tm:tm + CHUNK]

    o_ref[...] = h + jnp.dot(att_scr[...], wo_ref[...], preferred_element_type=F32)


def _swa_layer(h, mix_gain, w_qkv, q_gain, k_gain, sinks, w_o, bias_table):
    B, S, D = h.shape
    tm = min(TOKEN_TILE, S)
    dq = N_HEADS * HEAD_DIM
    dkv = N_KV_HEADS * HEAD_DIM
    lane_head = np.arange(QK_COL_TILE) // HEAD_DIM
    seg = jnp.asarray(lane_head[:, None] == lane_head[None, :], BF16)
    return pl.pallas_call(
        _swa_kernel,
        grid=(B, S // tm),
        in_specs=[
            _token_spec(tm, D),
            _const_spec((1, D)),
            _const_spec((D, dq + 2 * dkv)),
            _const_spec((1, dq)),
            _const_spec((1, dkv)),
            pl.BlockSpec(memory_space=pltpu.SMEM),
            _const_spec((N_KV_HEADS, 4 * CHUNK, 2 * CHUNK)),
            _const_spec((QK_COL_TILE, QK_COL_TILE)),
            _const_spec((dq, D)),
        ],
        out_specs=_token_spec(tm, D),
        out_shape=jax.ShapeDtypeStruct((B, S, D), F32),
        scratch_shapes=[
            pltpu.VMEM((CHUNK + tm, N_KV_HEADS * LANES), BF16),
            pltpu.VMEM((CHUNK + tm, N_KV_HEADS * LANES), BF16),
            pltpu.VMEM((N_KV_HEADS * 3 * HEAD_DIM, CHUNK + tm), BF16),
            pltpu.VMEM((tm, dq), BF16),
            pltpu.VMEM((tm, dq), BF16),
        ],
        compiler_params=_compiler_params(),
        name="swa_mixer",
    )(h, mix_gain.reshape(1, D), w_qkv.astype(BF16), jnp.tile(q_gain, N_HEADS).reshape(1, dq),
      jnp.tile(k_gain, N_KV_HEADS).reshape(1, dkv), sinks, bias_table, seg, w_o.astype(BF16))


def kernel(x, mix_norm, ffn_norm, sgu_w_in, sgu_v_gain, sgu_w_s, sgu_b_s, sgu_w_out, attn_w_qkv,
           attn_q_gain, attn_k_gain, attn_sinks, attn_w_o, rel_bias, ffn_w_up, ffn_conv_w, ffn_conv_b,
           ffn_w_down):
    depth = mix_norm.shape[0]
    bias_table = _bias_table(rel_bias) if depth > 1 else None
    h = x
    for i in range(depth):
        j = i // 2
        if i % 2 == 0:
            h = _sgu_layer(h, mix_norm[i], sgu_w_in[j], sgu_v_gain[j], sgu_w_s[j], sgu_b_s[j], sgu_w_out[j])
        else:
            h = _swa_layer(h, mix_norm[i], attn_w_qkv[j], attn_q_gain[j], attn_k_gain[j], attn_sinks[j],
                           attn_w_o[j], bias_table)
        h = _ffn_layer(h, ffn_norm[i], ffn_w_up[i], ffn_conv_w[i], ffn_conv_b[i], ffn_w_down[i])
    return h
```

```python
import functools
import math

import numpy as np
import jax
import jax.numpy as jnp
from jax import lax
from jax.experimental import pallas as pl
from jax.experimental.pallas import tpu as pltpu

F32 = jnp.float32
BF16 = jnp.bfloat16

EPS = 1e-6
LOG2_E = np.float32(1.0 / math.log(2.0))
LANES = 128
SUBLANES = 8
CHUNK = 128
HEAD_DIM = 64
N_HEADS = 16
N_KV_HEADS = 4
KV_GROUP = N_HEADS // N_KV_HEADS
SGU_GROUPS = 16
REL_BUCKETS = 32
REL_MAX_DIST = 128
CONV_WIDTH = 3

TOKEN_TILE = 512
FFN_TILES_PER_STEP = 2
SGU_COL_TILE = 512
FFN_COL_TILE = 256
QK_COL_TILE = 256
CHAINS_IN_FLIGHT = 4
VMEM_LIMIT_BYTES = 56 * 1024 * 1024


def _rms_norm(x, gain):
    ms = jnp.mean(x * x, axis=-1, keepdims=True)
    return x * lax.rsqrt(ms + EPS) * gain


def _interleave(chains, in_flight):
    waiting = list(chains)
    running, finished = [], set()
    while waiting or running:
        if waiting and len(running) < in_flight and all(id(p) in finished for p in waiting[0][1]):
            running.append(waiting.pop(0)[0])
        for chain in list(running):
            if next(chain, "done") == "done":
                running.remove(chain)
                finished.add(id(chain))


def _const_spec(shape):
    zeros = (0,) * len(shape)
    return pl.BlockSpec(shape, lambda b, j: zeros, pipeline_mode=pl.Buffered(1))


def _token_spec(tm, d):
    return pl.BlockSpec((None, tm, d), lambda b, j: (b, j, 0))


def _compiler_params():
    return pltpu.CompilerParams(
        dimension_semantics=("arbitrary", "arbitrary"),
        vmem_limit_bytes=VMEM_LIMIT_BYTES,
    )


def _sgu_kernel(h_ref, mg_ref, win_ref, vg_ref, ws_ref, bs_ref, wout_ref, o_ref,
                u_scr, v_scr, us_scr, wsm_scr):
    tm = h_ref.shape[0]
    width = u_scr.shape[1]

    @pl.when((pl.program_id(0) == 0) & (pl.program_id(1) == 0))
    def _():
        row = lax.broadcasted_iota(jnp.int32, (CHUNK, CHUNK), 0)
        col = lax.broadcasted_iota(jnp.int32, (CHUNK, CHUNK), 1)
        for g in range(SGU_GROUPS):
            wsm_scr[g] = jnp.where(row >= col, ws_ref[g], 0.0).astype(BF16)

    h = h_ref[...]
    hn = _rms_norm(h, mg_ref[...]).astype(BF16)

    v_sumsq = []

    def in_proj(c0):
        z = jnp.dot(hn, win_ref[:, c0:c0 + SGU_COL_TILE], preferred_element_type=F32)
        yield
        z = 0.5 * z * (1.0 + lax.erf(z * np.float32(math.sqrt(0.5))))
        if c0 < width:
            u_scr[:, c0:c0 + SGU_COL_TILE] = z
        else:
            v_sumsq.append(jnp.sum(z * z, axis=-1, keepdims=True))
            v_scr[:, c0 - width:c0 - width + SGU_COL_TILE] = z

    _interleave([(in_proj(c0), ()) for c0 in range(width, 2 * width, SGU_COL_TILE)], CHAINS_IN_FLIGHT)
    rstd = lax.rsqrt(sum(v_sumsq) * np.float32(1.0 / width) + EPS)

    n_chunk = tm // CHUNK

    def gate(g):
        g0 = g * LANES
        vn = jnp.concatenate(
            [(v_scr[cb * CHUNK:(cb + 1) * CHUNK, g0:g0 + LANES] * rstd[cb * CHUNK:(cb + 1) * CHUNK]
              * vg_ref[:, g0:g0 + LANES]).astype(BF16) for cb in range(n_chunk)], axis=1)
        yield
        s = jnp.dot(wsm_scr[g], vn, preferred_element_type=F32) + bs_ref[:, g:g + 1]
        yield
        for cb in range(n_chunk):
            r0 = cb * CHUNK
            us_scr[r0:r0 + CHUNK, g0:g0 + LANES] = (
                u_scr[r0:r0 + CHUNK, g0:g0 + LANES] * s[:, cb * CHUNK:(cb + 1) * CHUNK]).astype(BF16)

    u_projs = [in_proj(c0) for c0 in range(0, width, SGU_COL_TILE)]
    groups_per_tile = SGU_COL_TILE // LANES
    chains = [(u_projs[0], ())]
    for i, u_proj in enumerate(u_projs):
        if i + 1 < len(u_projs):
            chains.append((u_projs[i + 1], ()))
        chains.extend((gate(g), (u_proj,)) for g in range(i * groups_per_tile, (i + 1) * groups_per_tile))
    _interleave(chains, CHAINS_IN_FLIGHT)

    o_ref[...] = h + jnp.dot(us_scr[...], wout_ref[...], preferred_element_type=F32)


def _sgu_layer(h, mix_gain, w_in, v_gain, w_s, b_s, w_out):
    B, S, D = h.shape
    tm = min(TOKEN_TILE, S)
    width = w_out.shape[0]
    return pl.pallas_call(
        _sgu_kernel,
        grid=(B, S // tm),
        in_specs=[
            _token_spec(tm, D),
            _const_spec((1, D)),
            _const_spec((D, 2 * width)),
            _const_spec((1, width)),
            _const_spec((SGU_GROUPS, CHUNK, CHUNK)),
            _const_spec((CHUNK, SGU_GROUPS)),
            _const_spec((width, D)),
        ],
        out_specs=_token_spec(tm, D),
        out_shape=jax.ShapeDtypeStruct((B, S, D), F32),
        scratch_shapes=[
            pltpu.VMEM((tm, width), F32),
            pltpu.VMEM((tm, width), F32),
            pltpu.VMEM((tm, width), BF16),
            pltpu.VMEM((SGU_GROUPS, CHUNK, CHUNK), BF16),
        ],
        compiler_params=_compiler_params(),
        name="sgu_mixer",
    )(h, mix_gain.reshape(1, D), w_in.astype(BF16), v_gain.reshape(1, width), w_s,
      b_s.T, w_out.astype(BF16))


def _ffn_kernel(h_ref, ng_ref, wup_ref, cw_ref, cb_ref, wdown_ref, o_ref,
                carry_scr, act_scr, perm_scr):
    n_sub, tm, d_ff = act_scr.shape
    d = h_ref.shape[1]
    seg = tm // SUBLANES
    pitch = perm_scr.shape[2] // SUBLANES
    ct = FFN_COL_TILE
    n_col = d_ff // ct
    n_slab = d // LANES

    @pl.when(pl.program_id(1) == 0)
    def _():
        carry_scr[...] = jnp.zeros_like(carry_scr)

    first_row = lax.broadcasted_iota(jnp.int32, (seg, 2 * ct), 0) == 0
    h_pm, hn_pm, out_pm = {}, {}, {}

    def to_phase_major(i):
        for r in range(seg):
            for c in range(n_slab):
                perm_scr[i, c, pl.ds(r, SUBLANES, stride=pitch), :] = (
                    h_ref[i * tm + r * SUBLANES:i * tm + (r + 1) * SUBLANES, c * LANES:(c + 1) * LANES])
        yield
        h_pm[i] = jnp.concatenate(
            [jnp.concatenate([perm_scr[i, c, p * pitch:p * pitch + seg, :] for c in range(n_slab)], axis=1)
             for p in range(SUBLANES)], axis=0)
        hn_pm[i] = _rms_norm(h_pm[i], ng_ref[...]).astype(BF16)

    def pair(ref, rows, c):
        return jnp.concatenate([ref[rows, c * ct:(c + 1) * ct], ref[rows, d_ff + c * ct:d_ff + (c + 1) * ct]],
                               axis=1)

    def up_tile(i, c):
        c0 = 2 * ct * c
        a = jnp.dot(hn_pm[i], pair(wup_ref, slice(None), c), preferred_element_type=F32)
        yield
        wrap1 = jnp.where(first_row, carry_scr[1:2, c0:c0 + 2 * ct], pltpu.roll(a[7 * seg:8 * seg], 1, 0))
        wrap2 = jnp.where(first_row, carry_scr[0:1, c0:c0 + 2 * ct], pltpu.roll(a[6 * seg:7 * seg], 1, 0))
        carry_scr[0:1, c0:c0 + 2 * ct] = a[7 * seg - 1:7 * seg]
        carry_scr[1:2, c0:c0 + 2 * ct] = a[8 * seg - 1:8 * seg]
        a1 = jnp.concatenate([wrap1, a[0:7 * seg]], axis=0)
        a2 = jnp.concatenate([wrap2, wrap1, a[0:6 * seg]], axis=0)
        conv = (pair(cw_ref, slice(0, 1), c) * a2 + pair(cw_ref, slice(1, 2), c) * a1
                + pair(cw_ref, slice(2, 3), c) * a + pair(cb_ref, slice(None), c))
        yield
        gate = conv[:, 0:ct]
        val = conv[:, ct:2 * ct]
        act_scr[i, :, c * ct:(c + 1) * ct] = (gate / (1.0 + jnp.exp(-gate)) * val).astype(BF16)

    def down(i):
        out_pm[i] = h_pm[i] + jnp.dot(act_scr[i], wdown_ref[...], preferred_element_type=F32)
        yield

    def to_token_order(i):
        for p in range(SUBLANES):
            for c in range(n_slab):
                perm_scr[i, c, p * pitch:p * pitch + seg, :] = (
                    out_pm[i][p * seg:(p + 1) * seg, c * LANES:(c + 1) * LANES])
        yield
        for r in range(seg):
            for c in range(n_slab):
                o_ref[i * tm + r * SUBLANES:i * tm + (r + 1) * SUBLANES, c * LANES:(c + 1) * LANES] = (
                    perm_scr[i, c, pl.ds(r, SUBLANES, stride=pitch), :])

    prep = [to_phase_major(i) for i in range(n_sub)]
    ups = [[up_tile(i, c) for c in range(n_col)] for i in range(n_sub)]
    downs = [down(i) for i in range(n_sub)]
    chains = [(prep[0], ())]
    for i in range(n_sub):
        for c in range(n_col):
            if c == n_col // 2:
                if i + 1 < n_sub:
                    chains.append((prep[i + 1], ()))
                if i > 0:
                    chains.append((to_token_order(i - 1), (downs[i - 1],)))
            chains.append((ups[i][c], (prep[i],) + ((ups[i - 1][c],) if i > 0 else ())))
        chains.append((downs[i], tuple(ups[i])))
    chains.append((to_token_order(n_sub - 1), (downs[n_sub - 1],)))
    _interleave(chains, CHAINS_IN_FLIGHT)


def _ffn_layer(h, norm_gain, w_up, conv_w, conv_b, w_down):
    B, S, D = h.shape
    tm = min(TOKEN_TILE, S)
    n_sub = min(FFN_TILES_PER_STEP, S // tm)
    d_ff = w_down.shape[0]
    return pl.pallas_call(
        _ffn_kernel,
        grid=(B, S // (n_sub * tm)),
        in_specs=[
            _token_spec(n_sub * tm, D),
            _const_spec((1, D)),
            _const_spec((D, 2 * d_ff)),
            _const_spec((CONV_WIDTH, 2 * d_ff)),
            _const_spec((1, 2 * d_ff)),
            _const_spec((d_ff, D)),
        ],
        out_specs=_token_spec(n_sub * tm, D),
        out_shape=jax.ShapeDtypeStruct((B, S, D), F32),
        scratch_shapes=[
            pltpu.VMEM((SUBLANES, 2 * d_ff), F32),
            pltpu.VMEM((n_sub, tm, d_ff), BF16),
            pltpu.VMEM((n_sub, D // LANES, tm + SUBLANES * SUBLANES, LANES), F32),
        ],
        compiler_params=_compiler_params(),
        name="conv_glu_ffn",
    )(h, norm_gain.reshape(1, D), w_up.astype(BF16), conv_w, conv_b.reshape(1, 2 * d_ff),
      w_down.astype(BF16))


def _band_rel_buckets():
    dist = (np.arange(CHUNK)[:, None] + CHUNK) - np.arange(2 * CHUNK)[None, :]
    n = np.maximum(dist, 0)
    max_exact = REL_BUCKETS // 2
    large = max_exact + (np.log(np.maximum(n, 1).astype(np.float32) / max_exact)
                         / math.log(REL_MAX_DIST / max_exact)
                         * (REL_BUCKETS - max_exact)).astype(np.int32)
    large = np.minimum(large, REL_BUCKETS - 1)
    bucket = np.where(n < max_exact, n, large)
    return np.where((dist >= 0) & (dist < CHUNK), bucket, -1).astype(np.int32)


def _bias_kernel(bucket_ref, rb_ref, o_ref):
    head = pl.program_id(0)
    bucket = bucket_ref[...]
    acc = jnp.full(bucket.shape, -jnp.inf, F32)
    for b in range(REL_BUCKETS):
        acc = jnp.where(bucket == b, rb_ref[b, head], acc)
    o_ref[...] = acc.T * LOG2_E


def _bias_table(rel_bias):
    bucket = jnp.asarray(_band_rel_buckets())
    return pl.pallas_call(
        _bias_kernel,
        grid=(N_HEADS,),
        in_specs=[
            pl.BlockSpec((CHUNK, 2 * CHUNK), lambda h: (0, 0)),
            pl.BlockSpec(memory_space=pltpu.SMEM),
        ],
        out_specs=pl.BlockSpec((None, 2 * CHUNK, CHUNK),
                               lambda h: (h // KV_GROUP, h % 2, (h % KV_GROUP) // 2)),
        out_shape=jax.ShapeDtypeStruct((N_KV_HEADS, 4 * CHUNK, 2 * CHUNK), F32),
        name="rel_bias_table",
    )(bucket, rel_bias)


def _head_sumsq(x, seg):
    sq = x * x
    hi = sq.astype(BF16)
    lo = (sq - hi.astype(F32)).astype(BF16)
    return (jnp.dot(hi, seg, preferred_element_type=F32)
            + jnp.dot(lo, seg, preferred_element_type=F32))


def _swa_kernel(h_ref, mg_ref, wqkv_ref, qg_ref, kg_ref, sink_ref, bias_ref, seg_ref, wo_ref, o_ref,
                klo_scr, khi_scr, vt_scr, q_scr, att_scr):
    tm = h_ref.shape[0]
    dq = N_HEADS * HEAD_DIM
    dkv = N_KV_HEADS * HEAD_DIM
    first_tile = pl.program_id(1) == 0
    vt_rows = 3 * HEAD_DIM

    @pl.when(first_tile)
    def _():
        klo_scr[0:CHUNK, :] = jnp.zeros((CHUNK, klo_scr.shape[1]), BF16)
        khi_scr[0:CHUNK, :] = jnp.zeros((CHUNK, khi_scr.shape[1]), BF16)
        vt_scr[...] = jnp.zeros(vt_scr.shape, BF16)

    def store_lo_hi(lo_scr, hi_scr, x):
        zeros = jnp.zeros((tm, HEAD_DIM), F32)
        for kvh in range(N_KV_HEADS):
            x_h = x[:, kvh * HEAD_DIM:(kvh + 1) * HEAD_DIM]
            lo_scr[CHUNK:CHUNK + tm, kvh * LANES:(kvh + 1) * LANES] = (
                jnp.concatenate([x_h, zeros], axis=1).astype(BF16))
            hi_scr[CHUNK:CHUNK + tm, kvh * LANES:(kvh + 1) * LANES] = (
                jnp.concatenate([zeros, x_h], axis=1).astype(BF16))

    def store_transposed(x):
        xt = x.T
        for kvh in range(N_KV_HEADS):
            vt_scr[kvh * vt_rows + HEAD_DIM:kvh * vt_rows + 2 * HEAD_DIM, CHUNK:CHUNK + tm] = (
                xt[kvh * HEAD_DIM:(kvh + 1) * HEAD_DIM].astype(BF16))

    h = h_ref[...]
    hn = _rms_norm(h, mg_ref[...]).astype(BF16)
    seg = seg_ref[...]
    inv_hd = np.float32(1.0 / HEAD_DIM)

    def project(w0):
        xw = jnp.dot(hn, wqkv_ref[:, w0:w0 + 2 * QK_COL_TILE], preferred_element_type=F32)
        yield
        for c0 in range(w0, w0 + 2 * QK_COL_TILE, QK_COL_TILE):
            x = xw[:, c0 - w0:c0 - w0 + QK_COL_TILE]
            if c0 >= dq + dkv:
                store_transposed(x)
                yield
                continue
            r = lax.rsqrt(_head_sumsq(x, seg) * inv_hd + EPS)
            yield
            if c0 < dq:
                gain = qg_ref[:, c0:c0 + QK_COL_TILE] * (np.float32(HEAD_DIM ** -0.5) * LOG2_E)
                q_scr[:, c0:c0 + QK_COL_TILE] = (x * r * gain).astype(BF16)
            else:
                store_lo_hi(klo_scr, khi_scr, x * r * kg_ref[:, c0 - dq:c0 - dq + QK_COL_TILE])
            yield

    band_row = lax.broadcasted_iota(jnp.int32, (4 * CHUNK, 1), 0) % (2 * CHUNK)
    no_prev = jnp.where(first_tile & (band_row < CHUNK), -jnp.inf, 0.0).astype(F32)
    upper_pair = lax.broadcasted_iota(jnp.int32, (1, 2 * CHUNK), 1) >= CHUNK

    def scores(blk, kvh):
        r0 = blk * CHUNK
        c0 = kvh * KV_GROUP * HEAD_DIM
        q2 = jnp.concatenate([q_scr[r0:r0 + CHUNK, c0:c0 + LANES],
                              q_scr[r0:r0 + CHUNK, c0 + LANES:c0 + 2 * LANES]], axis=0)
        k2 = jnp.concatenate([klo_scr[r0:r0 + 2 * CHUNK, kvh * LANES:(kvh + 1) * LANES],
                              khi_scr[r0:r0 + 2 * CHUNK, kvh * LANES:(kvh + 1) * LANES]], axis=0)
        return lax.dot_general(k2, q2, (((1,), (1,)), ((), ())), preferred_element_type=F32)

    def attend(blk, kvh):
        r0 = blk * CHUNK
        c0 = kvh * KV_GROUP * HEAD_DIM
        s = scores(blk, kvh)
        yield
        s = s + bias_ref[kvh]
        if blk == 0:
            s = s + no_prev
        probs, inv_denoms = [], []
        for odd in range(2):
            t = s[odd * 2 * CHUNK:(odd + 1) * 2 * CHUNK]
            sink = LOG2_E * jnp.where(upper_pair, sink_ref[kvh * KV_GROUP + 2 + odd],
                                      sink_ref[kvh * KV_GROUP + odd])
            m = jnp.maximum(jnp.max(t, axis=0, keepdims=True), sink)
            yield
            p = jnp.exp2(t - m)
            inv_denoms.append(1.0 / (jnp.sum(p, axis=0, keepdims=True) + jnp.exp2(sink - m)))
            probs.append(p.astype(BF16))
            yield
        v0 = kvh * vt_rows
        vt2 = jnp.concatenate([vt_scr[v0 + HEAD_DIM:v0 + 3 * HEAD_DIM, r0:r0 + 2 * CHUNK],
                               vt_scr[v0:v0 + 2 * HEAD_DIM, r0:r0 + 2 * CHUNK]], axis=1)
        ot = jnp.dot(vt2, jnp.concatenate(probs, axis=0), preferred_element_type=F32)
        yield
        ot = jnp.concatenate([ot[0:HEAD_DIM] * inv_denoms[0], ot[HEAD_DIM:2 * HEAD_DIM] * inv_denoms[1]], axis=0)
        att_scr[r0:r0 + CHUNK, c0:c0 + LANES] = ot[:, 0:CHUNK].T.astype(BF16)
        att_scr[r0:r0 + CHUNK, c0 + LANES:c0 + 2 * LANES] = ot[:, CHUNK:2 * CHUNK].T.astype(BF16)

    kv_proj = project(dq)
    q_projs = [project(w0) for w0 in range(0, dq, 2 * QK_COL_TILE)]
    chains = [(kv_proj, ())] + [(q_proj, ()) for q_proj in q_projs]
    kvh_per_proj = 2 * QK_COL_TILE // (KV_GROUP * HEAD_DIM)
    for kvh in range(N_KV_HEADS):
        chains.extend((attend(blk, kvh), (kv_proj, q_projs[kvh // kvh_per_proj])) for blk in range(tm // CHUNK))
    _interleave(chains, CHAINS_IN_FLIGHT)

    klo_scr[0:CHUNK, :] = klo_scr[tm:tm + CHUNK, :]
    khi_scr[0:CHUNK, :] = khi_scr[tm:tm + CHUNK, :]
    vt_scr[:, 0:CHUNK] = vt_scr[:, tm:tm + CHUNK]

    o_ref[...] = h + jnp.dot(att_scr[...], wo_ref[...], preferred_element_type=F32)


def _swa_layer(h, mix_gain, w_qkv, q_gain, k_gain, sinks, w_o, bias_table):
    B, S, D = h.shape
    tm = min(TOKEN_TILE, S)
    dq = N_HEADS * HEAD_DIM
    dkv = N_KV_HEADS * HEAD_DIM
    lane_head = np.arange(QK_COL_TILE) // HEAD_DIM
    seg = jnp.asarray(lane_head[:, None] == lane_head[None, :], BF16)
    return pl.pallas_call(
        _swa_kernel,
        grid=(B, S // tm),
        in_specs=[
            _token_spec(tm, D),
            _const_spec((1, D)),
            _const_spec((D, dq + 2 * dkv)),
            _const_spec((1, dq)),
            _const_spec((1, dkv)),
            pl.BlockSpec(memory_space=pltpu.SMEM),
            _const_spec((N_KV_HEADS, 4 * CHUNK, 2 * CHUNK)),
            _const_spec((QK_COL_TILE, QK_COL_TILE)),
            _const_spec((dq, D)),
        ],
        out_specs=_token_spec(tm, D),
        out_shape=jax.ShapeDtypeStruct((B, S, D), F32),
        scratch_shapes=[
            pltpu.VMEM((CHUNK + tm, N_KV_HEADS * LANES), BF16),
            pltpu.VMEM((CHUNK + tm, N_KV_HEADS * LANES), BF16),
            pltpu.VMEM((N_KV_HEADS * 3 * HEAD_DIM, CHUNK + tm), BF16),
            pltpu.VMEM((tm, dq), BF16),
            pltpu.VMEM((tm, dq), BF16),
        ],
        compiler_params=_compiler_params(),
        name="swa_mixer",
    )(h, mix_gain.reshape(1, D), w_qkv.astype(BF16), jnp.tile(q_gain, N_HEADS).reshape(1, dq),
      jnp.tile(k_gain, N_KV_HEADS).reshape(1, dkv), sinks, bias_table, seg, w_o.astype(BF16))


def kernel(x, mix_norm, ffn_norm, sgu_w_in, sgu_v_gain, sgu_w_s, sgu_b_s, sgu_w_out, attn_w_qkv,
           attn_q_gain, attn_k_gain, attn_sinks, attn_w_o, rel_bias, ffn_w_up, ffn_conv_w, ffn_conv_b,
           ffn_w_down):
    depth = mix_norm.shape[0]
    bias_table = _bias_table(rel_bias) if depth > 1 else None
    h = x
    for i in range(depth):
        j = i // 2
        if i % 2 == 0:
            h = _sgu_layer(h, mix_norm[i], sgu_w_in[j], sgu_v_gain[j], sgu_w_s[j], sgu_b_s[j], sgu_w_out[j])
        else:
            h = _swa_layer(h, mix_norm[i], attn_w_qkv[j], attn_q_gain[j], attn_k_gain[j], attn_sinks[j],
                           attn_w_o[j], bias_table)
        h = _ffn_layer(h, ffn_norm[i], ffn_w_up[i], ffn_conv_w[i], ffn_conv_b[i], ffn_w_down[i])
    return h
```

```python
import functools
import math

import numpy as np
import jax
import jax.numpy as jnp
from jax import lax
from jax.experimental import pallas as pl
from jax.experimental.pallas import tpu as pltpu

F32 = jnp.float32
BF16 = jnp.bfloat16

EPS = 1e-6
LOG2_E = np.float32(1.0 / math.log(2.0))
LANES = 128
SUBLANES = 8
CHUNK = 128
HEAD_DIM = 64
N_HEADS = 16
N_KV_HEADS = 4
KV_GROUP = N_HEADS // N_KV_HEADS
SGU_GROUPS = 16
REL_BUCKETS = 32
REL_MAX_DIST = 128
CONV_WIDTH = 3

TOKEN_TILE = 512
FFN_TILES_PER_STEP = 1
W_TILE_COLS = 256
SGU_COL_TILE = 512
FFN_COL_TILE = 256
QK_COL_TILE = 256
CHAINS_IN_FLIGHT = 4
VMEM_LIMIT_BYTES = 56 * 1024 * 1024


def _rms_norm(x, gain):
    ms = jnp.mean(x * x, axis=-1, keepdims=True)
    return x * lax.rsqrt(ms + EPS) * gain


def _interleave(chains, in_flight):
    waiting = list(chains)
    running, finished = [], set()
    while waiting or running:
        if waiting and len(running) < in_flight and all(id(p) in finished for p in waiting[0][1]):
            running.append(waiting.pop(0)[0])
        for chain in list(running):
            if next(chain, "done") == "done":
                running.remove(chain)
                finished.add(id(chain))


def _fetch_as_bf16(src_hbm, dst_scr, stage, sems, tiles):
    def copy(k):
        (src_rows, src_cols), _ = tiles[k]
        return pltpu.make_async_copy(src_hbm.at[src_rows, src_cols], stage.at[k % 2], sems.at[k % 2])

    copy(0).start()
    for k, (_, (dst_rows, dst_cols)) in enumerate(tiles):
        if k + 1 < len(tiles):
            copy(k + 1).start()
        copy(k).wait()
        dst_scr[dst_rows, dst_cols] = stage[k % 2].astype(BF16)


def _col_tiles(n_rows, n_cols):
    return [((slice(0, n_rows), slice(c, c + W_TILE_COLS)),) * 2 for c in range(0, n_cols, W_TILE_COLS)]


def _row_tiles(n_rows, n_cols):
    return [((slice(r, r + W_TILE_COLS), slice(0, n_cols)),) * 2 for r in range(0, n_rows, W_TILE_COLS)]


def _weight_scratch(d_model):
    return [
        pltpu.VMEM((2, d_model, W_TILE_COLS), F32),
        pltpu.VMEM((2, W_TILE_COLS, d_model), F32),
        pltpu.SemaphoreType.DMA((2,)),
    ]


def _const_spec(shape):
    zeros = (0,) * len(shape)
    return pl.BlockSpec(shape, lambda b, j: zeros, pipeline_mode=pl.Buffered(1))


def _token_spec(tm, d):
    return pl.BlockSpec((None, tm, d), lambda b, j: (b, j, 0))


def _compiler_params():
    return pltpu.CompilerParams(
        dimension_semantics=("arbitrary", "arbitrary"),
        vmem_limit_bytes=VMEM_LIMIT_BYTES,
    )


def _sgu_kernel(layer, h_ref, mg_ref, win_hbm, vg_ref, ws_ref, bs_ref, wout_hbm, o_ref,
                u_scr, v_scr, us_scr, wsm_scr, win_ref, wout_ref, tall_stage, wide_stage, dma_sems):
    tm = h_ref.shape[0]
    width = u_scr.shape[1]

    @pl.when((pl.program_id(0) == 0) & (pl.program_id(1) == 0))
    def _():
        _fetch_as_bf16(win_hbm.at[layer], win_ref, tall_stage, dma_sems, _col_tiles(*win_ref.shape))
        _fetch_as_bf16(wout_hbm.at[layer], wout_ref, wide_stage, dma_sems, _row_tiles(*wout_ref.shape))
        row = lax.broadcasted_iota(jnp.int32, (CHUNK, CHUNK), 0)
        col = lax.broadcasted_iota(jnp.int32, (CHUNK, CHUNK), 1)
        for g in range(SGU_GROUPS):
            wsm_scr[g] = jnp.where(row >= col, ws_ref[g], 0.0).astype(BF16)

    h = h_ref[...]
    hn = _rms_norm(h, mg_ref[...]).astype(BF16)

    v_sumsq = []

    def in_proj(c0):
        z = jnp.dot(hn, win_ref[:, c0:c0 + SGU_COL_TILE], preferred_element_type=F32)
        yield
        z = 0.5 * z * (1.0 + lax.erf(z * np.float32(math.sqrt(0.5))))
        if c0 < width:
            u_scr[:, c0:c0 + SGU_COL_TILE] = z
        else:
            v_sumsq.append(jnp.sum(z * z, axis=-1, keepdims=True))
            v_scr[:, c0 - width:c0 - width + SGU_COL_TILE] = z

    _interleave([(in_proj(c0), ()) for c0 in range(width, 2 * width, SGU_COL_TILE)], CHAINS_IN_FLIGHT)
    rstd = lax.rsqrt(sum(v_sumsq) * np.float32(1.0 / width) + EPS)

    n_chunk = tm // CHUNK

    def gate(g):
        g0 = g * LANES
        vn = jnp.concatenate(
            [(v_scr[cb * CHUNK:(cb + 1) * CHUNK, g0:g0 + LANES] * rstd[cb * CHUNK:(cb + 1) * CHUNK]
              * vg_ref[:, g0:g0 + LANES]).astype(BF16) for cb in range(n_chunk)], axis=1)
        yield
        s = jnp.dot(wsm_scr[g], vn, preferred_element_type=F32) + bs_ref[:, g:g + 1]
        yield
        for cb in range(n_chunk):
            r0 = cb * CHUNK
            us_scr[r0:r0 + CHUNK, g0:g0 + LANES] = (
                u_scr[r0:r0 + CHUNK, g0:g0 + LANES] * s[:, cb * CHUNK:(cb + 1) * CHUNK]).astype(BF16)

    u_projs = [in_proj(c0) for c0 in range(0, width, SGU_COL_TILE)]
    groups_per_tile = SGU_COL_TILE // LANES
    chains = [(u_projs[0], ())]
    for i, u_proj in enumerate(u_projs):
        if i + 1 < len(u_projs):
            chains.append((u_projs[i + 1], ()))
        chains.extend((gate(g), (u_proj,)) for g in range(i * groups_per_tile, (i + 1) * groups_per_tile))
    _interleave(chains, CHAINS_IN_FLIGHT)

    o_ref[...] = h + jnp.dot(us_scr[...], wout_ref[...], preferred_element_type=F32)


def _sgu_layer(layer, h, mix_gain, w_in_all, v_gain, w_s, b_s, w_out_all):
    B, S, D = h.shape
    tm = min(TOKEN_TILE, S)
    width = w_out_all.shape[1]
    return pl.pallas_call(
        functools.partial(_sgu_kernel, layer),
        grid=(B, S // tm),
        in_specs=[
            _token_spec(tm, D),
            _const_spec((1, D)),
            pl.BlockSpec(memory_space=pl.ANY),
            _const_spec((1, width)),
            _const_spec((SGU_GROUPS, CHUNK, CHUNK)),
            _const_spec((CHUNK, SGU_GROUPS)),
            pl.BlockSpec(memory_space=pl.ANY),
        ],
        out_specs=_token_spec(tm, D),
        out_shape=jax.ShapeDtypeStruct((B, S, D), F32),
        scratch_shapes=[
            pltpu.VMEM((tm, width), F32),
            pltpu.VMEM((tm, width), F32),
            pltpu.VMEM((tm, width), BF16),
            pltpu.VMEM((SGU_GROUPS, CHUNK, CHUNK), BF16),
            pltpu.VMEM((D, 2 * width), BF16),
            pltpu.VMEM((width, D), BF16),
        ] + _weight_scratch(D),
        compiler_params=_compiler_params(),
        name="sgu_mixer",
    )(h, mix_gain.reshape(1, D), w_in_all, v_gain.reshape(1, width), w_s, b_s.T, w_out_all)


def _ffn_kernel(layer, h_ref, ng_ref, wup_hbm, cw_ref, cb_ref, wdown_hbm, o_ref,
                carry_scr, act_scr, perm_scr, wup_ref, wdown_ref, tall_stage, wide_stage, dma_sems):
    n_sub, tm, d_ff = act_scr.shape
    d = h_ref.shape[1]
    seg = tm // SUBLANES
    pitch = perm_scr.shape[2] // SUBLANES
    ct = FFN_COL_TILE
    n_col = d_ff // ct
    n_slab = d // LANES

    @pl.when((pl.program_id(0) == 0) & (pl.program_id(1) == 0))
    def _():
        _fetch_as_bf16(wup_hbm.at[layer], wup_ref, tall_stage, dma_sems, _col_tiles(*wup_ref.shape))
        _fetch_as_bf16(wdown_hbm.at[layer], wdown_ref, wide_stage, dma_sems, _row_tiles(*wdown_ref.shape))

    @pl.when(pl.program_id(1) == 0)
    def _():
        carry_scr[...] = jnp.zeros_like(carry_scr)

    first_row = lax.broadcasted_iota(jnp.int32, (seg, 2 * ct), 0) == 0
    h_pm, hn_pm, out_pm = {}, {}, {}

    def to_phase_major(i):
        for r in range(seg):
            for c in range(n_slab):
                perm_scr[i, c, pl.ds(r, SUBLANES, stride=pitch), :] = (
                    h_ref[i * tm + r * SUBLANES:i * tm + (r + 1) * SUBLANES, c * LANES:(c + 1) * LANES])
        yield
        h_pm[i] = jnp.concatenate(
            [jnp.concatenate([perm_scr[i, c, p * pitch:p * pitch + seg, :] for c in range(n_slab)], axis=1)
             for p in range(SUBLANES)], axis=0)
        hn_pm[i] = _rms_norm(h_pm[i], ng_ref[...]).astype(BF16)

    def pair(ref, rows, c):
        return jnp.concatenate([ref[rows, c * ct:(c + 1) * ct], ref[rows, d_ff + c * ct:d_ff + (c + 1) * ct]],
                               axis=1)

    def up_tile(i, c):
        c0 = 2 * ct * c
        a = jnp.dot(hn_pm[i], pair(wup_ref, slice(None), c), preferred_element_type=F32)
        yield
        wrap1 = jnp.where(first_row, carry_scr[1:2, c0:c0 + 2 * ct], pltpu.roll(a[7 * seg:8 * seg], 1, 0))
        wrap2 = jnp.where(first_row, carry_scr[0:1, c0:c0 + 2 * ct], pltpu.roll(a[6 * seg:7 * seg], 1, 0))
        carry_scr[0:1, c0:c0 + 2 * ct] = a[7 * seg - 1:7 * seg]
        carry_scr[1:2, c0:c0 + 2 * ct] = a[8 * seg - 1:8 * seg]
        a1 = jnp.concatenate([wrap1, a[0:7 * seg]], axis=0)
        a2 = jnp.concatenate([wrap2, wrap1, a[0:6 * seg]], axis=0)
        conv = (pair(cw_ref, slice(0, 1), c) * a2 + pair(cw_ref, slice(1, 2), c) * a1
                + pair(cw_ref, slice(2, 3), c) * a + pair(cb_ref, slice(None), c))
        yield
        gate = conv[:, 0:ct]
        val = conv[:, ct:2 * ct]
        act_scr[i, :, c * ct:(c + 1) * ct] = (gate / (1.0 + jnp.exp(-gate)) * val).astype(BF16)

    def down(i):
        out_pm[i] = h_pm[i] + jnp.dot(act_scr[i], wdown_ref[...], preferred_element_type=F32)
        yield

    def to_token_order(i):
        for p in range(SUBLANES):
            for c in range(n_slab):
                perm_scr[i, c, p * pitch:p * pitch + seg, :] = (
                    out_pm[i][p * seg:(p + 1) * seg, c * LANES:(c + 1) * LANES])
        yield
        for r in range(seg):
            for c in range(n_slab):
                o_ref[i * tm + r * SUBLANES:i * tm + (r + 1) * SUBLANES, c * LANES:(c + 1) * LANES] = (
                    perm_scr[i, c, pl.ds(r, SUBLANES, stride=pitch), :])

    prep = [to_phase_major(i) for i in range(n_sub)]
    ups = [[up_tile(i, c) for c in range(n_col)] for i in range(n_sub)]
    downs = [down(i) for i in range(n_sub)]
    chains = [(prep[0], ())]
    for i in range(n_sub):
        for c in range(n_col):
            if c == n_col // 2 and i + 1 < n_sub:
                chains.append((prep[i + 1], ()))
            chains.append((ups[i][c], (prep[i],) + ((ups[i - 1][c],) if i > 0 else ())))
    for i in range(n_sub):
        chains.append((downs[i], tuple(ups[i])))
        if i > 0:
            chains.append((to_token_order(i - 1), (downs[i - 1],)))
    chains.append((to_token_order(n_sub - 1), (downs[n_sub - 1],)))
    _interleave(chains, CHAINS_IN_FLIGHT)


def _ffn_layer(layer, h, norm_gain, w_up_all, conv_w, conv_b, w_down_all):
    B, S, D = h.shape
    tm = min(TOKEN_TILE, S)
    n_sub = min(FFN_TILES_PER_STEP, S // tm)
    d_ff = w_down_all.shape[1]
    return pl.pallas_call(
        functools.partial(_ffn_kernel, layer),
        grid=(B, S // (n_sub * tm)),
        in_specs=[
            _token_spec(n_sub * tm, D),
            _const_spec((1, D)),
            pl.BlockSpec(memory_space=pl.ANY),
            _const_spec((CONV_WIDTH, 2 * d_ff)),
            _const_spec((1, 2 * d_ff)),
            pl.BlockSpec(memory_space=pl.ANY),
        ],
        out_specs=_token_spec(n_sub * tm, D),
        out_shape=jax.ShapeDtypeStruct((B, S, D), F32),
        scratch_shapes=[
            pltpu.VMEM((SUBLANES, 2 * d_ff), F32),
            pltpu.VMEM((n_sub, tm, d_ff), BF16),
            pltpu.VMEM((n_sub, D // LANES, tm + SUBLANES * SUBLANES, LANES), F32),
            pltpu.VMEM((D, 2 * d_ff), BF16),
            pltpu.VMEM((d_ff, D), BF16),
        ] + _weight_scratch(D),
        compiler_params=_compiler_params(),
        name="conv_glu_ffn",
    )(h, norm_gain.reshape(1, D), w_up_all, conv_w, conv_b.reshape(1, 2 * d_ff), w_down_all)


def _band_rel_buckets():
    dist = (np.arange(CHUNK)[:, None] + CHUNK) - np.arange(2 * CHUNK)[None, :]
    n = np.maximum(dist, 0)
    max_exact = REL_BUCKETS // 2
    large = max_exact + (np.log(np.maximum(n, 1).astype(np.float32) / max_exact)
                         / math.log(REL_MAX_DIST / max_exact)
                         * (REL_BUCKETS - max_exact)).astype(np.int32)
    large = np.minimum(large, REL_BUCKETS - 1)
    bucket = np.where(n < max_exact, n, large)
    return np.where((dist >= 0) & (dist < CHUNK), bucket, -1).astype(np.int32)


def _bias_kernel(bucket_ref, rb_ref, o_ref):
    head = pl.program_id(0)
    bucket = bucket_ref[...]
    acc = jnp.full(bucket.shape, -jnp.inf, F32)
    for b in range(REL_BUCKETS):
        acc = jnp.where(bucket == b, rb_ref[b, head], acc)
    o_ref[...] = acc.T * LOG2_E


def _bias_table(rel_bias):
    bucket = jnp.asarray(_band_rel_buckets())
    return pl.pallas_call(
        _bias_kernel,
        grid=(N_HEADS,),
        in_specs=[
            pl.BlockSpec((CHUNK, 2 * CHUNK), lambda h: (0, 0)),
            pl.BlockSpec(memory_space=pltpu.SMEM),
        ],
        out_specs=pl.BlockSpec((None, 2 * CHUNK, CHUNK),
                               lambda h: (h // KV_GROUP, h % 2, (h % KV_GROUP) // 2)),
        out_shape=jax.ShapeDtypeStruct((N_KV_HEADS, 4 * CHUNK, 2 * CHUNK), F32),
        name="rel_bias_table",
    )(bucket, rel_bias)


def _head_sumsq(x, seg):
    sq = x * x
    hi = sq.astype(BF16)
    lo = (sq - hi.astype(F32)).astype(BF16)
    return (jnp.dot(hi, seg, preferred_element_type=F32)
            + jnp.dot(lo, seg, preferred_element_type=F32))


def _swa_kernel(layer, h_ref, mg_ref, wqkv_hbm, qg_ref, kg_ref, sink_ref, bias_ref, seg_ref, wo_hbm, o_ref,
                klo_scr, khi_scr, vt_scr, q_scr, att_scr, wqkv_ref, wo_ref, tall_stage, wide_stage, dma_sems):
    tm = h_ref.shape[0]
    dq = N_HEADS * HEAD_DIM
    dkv = N_KV_HEADS * HEAD_DIM
    first_tile = pl.program_id(1) == 0
    vt_rows = 3 * HEAD_DIM

    @pl.when((pl.program_id(0) == 0) & first_tile)
    def _():
        _fetch_as_bf16(wqkv_hbm.at[layer], wqkv_ref, tall_stage, dma_sems, _col_tiles(*wqkv_ref.shape))
        _fetch_as_bf16(wo_hbm.at[layer], wo_ref, wide_stage, dma_sems, _row_tiles(*wo_ref.shape))

    @pl.when(first_tile)
    def _():
        klo_scr[0:CHUNK, :] = jnp.zeros((CHUNK, klo_scr.shape[1]), BF16)
        khi_scr[0:CHUNK, :] = jnp.zeros((CHUNK, khi_scr.shape[1]), BF16)
        vt_scr[...] = jnp.zeros(vt_scr.shape, BF16)

    def store_lo_hi(lo_scr, hi_scr, x):
        zeros = jnp.zeros((tm, HEAD_DIM), F32)
        for kvh in range(N_KV_HEADS):
            x_h = x[:, kvh * HEAD_DIM:(kvh + 1) * HEAD_DIM]
            lo_scr[CHUNK:CHUNK + tm, kvh * LANES:(kvh + 1) * LANES] = (
                jnp.concatenate([x_h, zeros], axis=1).astype(BF16))
            hi_scr[CHUNK:CHUNK + tm, kvh * LANES:(kvh + 1) * LANES] = (
                jnp.concatenate([zeros, x_h], axis=1).astype(BF16))

    def store_transposed(x):
        xt = x.T
        for kvh in range(N_KV_HEADS):
            vt_scr[kvh * vt_rows + HEAD_DIM:kvh * vt_rows + 2 * HEAD_DIM, CHUNK:CHUNK + tm] = (
                xt[kvh * HEAD_DIM:(kvh + 1) * HEAD_DIM].astype(BF16))

    h = h_ref[...]
    hn = _rms_norm(h, mg_ref[...]).astype(BF16)
    seg = seg_ref[...]
    inv_hd = np.float32(1.0 / HEAD_DIM)

    def project(w0):
        xw = jnp.dot(hn, wqkv_ref[:, w0:w0 + 2 * QK_COL_TILE], preferred_element_type=F32)
        yield
        for c0 in range(w0, w0 + 2 * QK_COL_TILE, QK_COL_TILE):
            x = xw[:, c0 - w0:c0 - w0 + QK_COL_TILE]
            if c0 >= dq + dkv:
                store_transposed(x)
                yield
                continue
            r = lax.rsqrt(_head_sumsq(x, seg) * inv_hd + EPS)
            yield
            if c0 < dq:
                gain = qg_ref[:, c0:c0 + QK_COL_TILE] * (np.float32(HEAD_DIM ** -0.5) * LOG2_E)
                q_scr[:, c0:c0 + QK_COL_TILE] = (x * r * gain).astype(BF16)
            else:
                store_lo_hi(klo_scr, khi_scr, x * r * kg_ref[:, c0 - dq:c0 - dq + QK_COL_TILE])
            yield

    band_row = lax.broadcasted_iota(jnp.int32, (4 * CHUNK, 1), 0) % (2 * CHUNK)
    no_prev = jnp.where(first_tile & (band_row < CHUNK), -jnp.inf, 0.0).astype(F32)
    upper_pair = lax.broadcasted_iota(jnp.int32, (1, 2 * CHUNK), 1) >= CHUNK

    def scores(blk, kvh):
        r0 = blk * CHUNK
        c0 = kvh * KV_GROUP * HEAD_DIM
        q2 = jnp.concatenate([q_scr[r0:r0 + CHUNK, c0:c0 + LANES],
                              q_scr[r0:r0 + CHUNK, c0 + LANES:c0 + 2 * LANES]], axis=0)
        k2 = jnp.concatenate([klo_scr[r0:r0 + 2 * CHUNK, kvh * LANES:(kvh + 1) * LANES],
                              khi_scr[r0:r0 + 2 * CHUNK, kvh * LANES:(kvh + 1) * LANES]], axis=0)
        return lax.dot_general(k2, q2, (((1,), (1,)), ((), ())), preferred_element_type=F32)

    def attend(blk, kvh):
        r0 = blk * CHUNK
        c0 = kvh * KV_GROUP * HEAD_DIM
        s = scores(blk, kvh)
        yield
        s = s + bias_ref[kvh]
        if blk == 0:
            s = s + no_prev
        probs, inv_denoms = [], []
        for odd in range(2):
            t = s[odd * 2 * CHUNK:(odd + 1) * 2 * CHUNK]
            sink = LOG2_E * jnp.where(upper_pair, sink_ref[kvh * KV_GROUP + 2 + odd],
                                      sink_ref[kvh * KV_GROUP + odd])
            m = jnp.maximum(jnp.max(t, axis=0, keepdims=True), sink)
            yield
            p = jnp.exp2(t - m)
            inv_denoms.append(1.0 / (jnp.sum(p, axis=0, keepdims=True) + jnp.exp2(sink - m)))
            probs.append(p.astype(BF16))
            yield
        v0 = kvh * vt_rows
        vt2 = jnp.concatenate([vt_scr[v0 + HEAD_DIM:v0 + 3 * HEAD_DIM, r0:r0 + 2 * CHUNK],
                               vt_scr[v0:v0 + 2 * HEAD_DIM, r0:r0 + 2 * CHUNK]], axis=1)
        ot = jnp.dot(vt2, jnp.concatenate(probs, axis=0), preferred_element_type=F32)
        yield
        ot = jnp.concatenate([ot[0:HEAD_DIM] * inv_denoms[0], ot[HEAD_DIM:2 * HEAD_DIM] * inv_denoms[1]], axis=0)
        att_scr[r0:r0 + CHUNK, c0:c0 + LANES] = ot[:, 0:CHUNK].T.astype(BF16)
        att_scr[r0:r0 + CHUNK, c0 + LANES:c0 + 2 * LANES] = ot[:, CHUNK:2 * CHUNK].T.astype(BF16)

    kv_proj = project(dq)
    q_projs = [project(w0) for w0 in range(0, dq, 2 * QK_COL_TILE)]
    chains = [(kv_proj, ())] + [(q_proj, ()) for q_proj in q_projs]
    kvh_per_proj = 2 * QK_COL_TILE // (KV_GROUP * HEAD_DIM)
    for kvh in range(N_KV_HEADS):
        chains.extend((attend(blk, kvh), (kv_proj, q_projs[kvh // kvh_per_proj])) for blk in range(tm // CHUNK))
    _interleave(chains, CHAINS_IN_FLIGHT)

    klo_scr[0:CHUNK, :] = klo_scr[tm:tm + CHUNK, :]
    khi_scr[0:CHUNK, :] = khi_scr[tm:tm + CHUNK, :]
    vt_scr[:, 0:CHUNK] = vt_scr[:, tm:tm + CHUNK]

    o_ref[...] = h + jnp.dot(att_scr[...], wo_ref[...], preferred_element_type=F32)


def _swa_layer(layer, h, mix_gain, w_qkv_all, q_gain, k_gain, sinks, w_o_all, bias_table):
    B, S, D = h.shape
    tm = min(TOKEN_TILE, S)
    dq = N_HEADS * HEAD_DIM
    dkv = N_KV_HEADS * HEAD_DIM
    lane_head = np.arange(QK_COL_TILE) // HEAD_DIM
    seg = jnp.asarray(lane_head[:, None] == lane_head[None, :], BF16)
    return pl.pallas_call(
        functools.partial(_swa_kernel, layer),
        grid=(B, S // tm),
        in_specs=[
            _token_spec(tm, D),
            _const_spec((1, D)),
            pl.BlockSpec(memory_space=pl.ANY),
            _const_spec((1, dq)),
            _const_spec((1, dkv)),
            pl.BlockSpec(memory_space=pltpu.SMEM),
            _const_spec((N_KV_HEADS, 4 * CHUNK, 2 * CHUNK)),
            _const_spec((QK_COL_TILE, QK_COL_TILE)),
            pl.BlockSpec(memory_space=pl.ANY),
        ],
        out_specs=_token_spec(tm, D),
        out_shape=jax.ShapeDtypeStruct((B, S, D), F32),
        scratch_shapes=[
            pltpu.VMEM((CHUNK + tm, N_KV_HEADS * LANES), BF16),
            pltpu.VMEM((CHUNK + tm, N_KV_HEADS * LANES), BF16),
            pltpu.VMEM((N_KV_HEADS * 3 * HEAD_DIM, CHUNK + tm), BF16),
            pltpu.VMEM((tm, dq), BF16),
            pltpu.VMEM((tm, dq), BF16),
            pltpu.VMEM((D, dq + 2 * dkv), BF16),
            pltpu.VMEM((dq, D), BF16),
        ] + _weight_scratch(D),
        compiler_params=_compiler_params(),
        name="swa_mixer",
    )(h, mix_gain.reshape(1, D), w_qkv_all, jnp.tile(q_gain, N_HEADS).reshape(1, dq),
      jnp.tile(k_gain, N_KV_HEADS).reshape(1, dkv), sinks, bias_table, seg, w_o_all)


def kernel(x, mix_norm, ffn_norm, sgu_w_in, sgu_v_gain, sgu_w_s, sgu_b_s, sgu_w_out, attn_w_qkv,
           attn_q_gain, attn_k_gain, attn_sinks, attn_w_o, rel_bias, ffn_w_up, ffn_conv_w, ffn_conv_b,
           ffn_w_down):
    depth = mix_norm.shape[0]
    bias_table = _bias_table(rel_bias) if depth > 1 else None
    h = x
    for i in range(depth):
        j = i // 2
        if i % 2 == 0:
            h = _sgu_layer(j, h, mix_norm[i], sgu_w_in, sgu_v_gain[j], sgu_w_s[j], sgu_b_s[j], sgu_w_out)
        else:
            h = _swa_layer(j, h, mix_norm[i], attn_w_qkv, attn_q_gain[j], attn_k_gain[j], attn_sinks[j],
                           attn_w_o, bias_table)
        h = _ffn_layer(i, h, ffn_norm[i], ffn_w_up, ffn_conv_w[i], ffn_conv_b[i], ffn_w_down)
    return h
```

```python
import functools
import math

import numpy as np
import jax
import jax.numpy as jnp
from jax import lax
from jax.experimental import pallas as pl
from jax.experimental.pallas import tpu as pltpu

F32 = jnp.float32
BF16 = jnp.bfloat16

EPS = 1e-6
LOG2_E = np.float32(1.0 / math.log(2.0))
LANES = 128
SUBLANES = 8
CHUNK = 128
HEAD_DIM = 64
N_HEADS = 16
N_KV_HEADS = 4
KV_GROUP = N_HEADS // N_KV_HEADS
SGU_GROUPS = 16
REL_BUCKETS = 32
REL_MAX_DIST = 128
CONV_WIDTH = 3

TOKEN_TILE = 512
FFN_TILES_PER_STEP = 1
FETCH_SLOTS = 3
FETCH_MAX_TILE_ROWS = 256
FETCH_MAX_TILE_BYTES = 3 * 512 * 1024
SGU_COL_TILE = 512
FFN_COL_TILE = 256
QK_COL_TILE = 256
CHAINS_IN_FLIGHT = 6
VMEM_LIMIT_BYTES = 56 * 1024 * 1024


def _rms_norm(x, gain):
    ms = jnp.mean(x * x, axis=-1, keepdims=True)
    return x * lax.rsqrt(ms + EPS) * gain


def _interleave(chains, in_flight):
    waiting = list(chains)
    running, finished = [], set()
    while waiting or running:
        if waiting and len(running) < in_flight and all(id(p) in finished for p in waiting[0][1]):
            running.append(waiting.pop(0)[0])
        for chain in list(running):
            if next(chain, "done") == "done":
                running.remove(chain)
                finished.add(id(chain))


def _fetch_as_bf16(src_hbm, dst_scr, stage, sems):
    n_slots, tile_rows, _ = stage.shape
    n_tiles = dst_scr.shape[0] // tile_rows

    def copy(k):
        return pltpu.make_async_copy(src_hbm.at[pl.ds(k * tile_rows, tile_rows), :],
                                     stage.at[k % n_slots], sems.at[k % n_slots])

    for k in range(min(n_slots - 1, n_tiles)):
        copy(k).start()
    for k in range(n_tiles):
        if k + n_slots - 1 < n_tiles:
            copy(k + n_slots - 1).start()
        copy(k).wait()
        dst_scr[k * tile_rows:(k + 1) * tile_rows, :] = stage[k % n_slots].astype(BF16)


def _weight_scratch(n_rows, n_cols):
    tile_rows = FETCH_MAX_TILE_ROWS
    while tile_rows * n_cols * 4 > FETCH_MAX_TILE_BYTES or n_rows % tile_rows:
        tile_rows //= 2
    assert tile_rows >= SUBLANES, (n_rows, n_cols)
    return [
        pltpu.VMEM((n_rows, n_cols), BF16),
        pltpu.VMEM((FETCH_SLOTS, tile_rows, n_cols), F32),
        pltpu.SemaphoreType.DMA((FETCH_SLOTS,)),
    ]


def _const_spec(shape):
    zeros = (0,) * len(shape)
    return pl.BlockSpec(shape, lambda b, j: zeros, pipeline_mode=pl.Buffered(1))


def _token_spec(tm, d):
    return pl.BlockSpec((None, tm, d), lambda b, j: (b, j, 0))


def _compiler_params():
    return pltpu.CompilerParams(
        dimension_semantics=("arbitrary", "arbitrary"),
        vmem_limit_bytes=VMEM_LIMIT_BYTES,
    )


def _sgu_kernel(layer, h_ref, mg_ref, win_hbm, vg_ref, ws_ref, bs_ref, wout_hbm, o_ref,
                u_scr, v_scr, us_scr, wsm_scr, win_ref, win_stage, win_sems, wout_ref, wout_stage, wout_sems):
    tm = h_ref.shape[0]
    width = u_scr.shape[1]

    @pl.when((pl.program_id(0) == 0) & (pl.program_id(1) == 0))
    def _():
        _fetch_as_bf16(win_hbm.at[layer], win_ref, win_stage, win_sems)
        _fetch_as_bf16(wout_hbm.at[layer], wout_ref, wout_stage, wout_sems)
        row = lax.broadcasted_iota(jnp.int32, (CHUNK, CHUNK), 0)
        col = lax.broadcasted_iota(jnp.int32, (CHUNK, CHUNK), 1)
        for g in range(SGU_GROUPS):
            wsm_scr[g] = jnp.where(row >= col, ws_ref[g], 0.0).astype(BF16)

    h = h_ref[...]
    hn = _rms_norm(h, mg_ref[...]).astype(BF16)

    v_sumsq = []

    def in_proj(c0):
        z = jnp.dot(hn, win_ref[:, c0:c0 + SGU_COL_TILE], preferred_element_type=F32)
        yield
        z = 0.5 * z * (1.0 + lax.erf(z * np.float32(math.sqrt(0.5))))
        if c0 < width:
            u_scr[:, c0:c0 + SGU_COL_TILE] = z
        else:
            v_sumsq.append(jnp.sum(z * z, axis=-1, keepdims=True))
            v_scr[:, c0 - width:c0 - width + SGU_COL_TILE] = z

    _interleave([(in_proj(c0), ()) for c0 in range(width, 2 * width, SGU_COL_TILE)], CHAINS_IN_FLIGHT)
    rstd = lax.rsqrt(sum(v_sumsq) * np.float32(1.0 / width) + EPS)

    n_chunk = tm // CHUNK

    def gate(g):
        g0 = g * LANES
        vn = jnp.concatenate(
            [(v_scr[cb * CHUNK:(cb + 1) * CHUNK, g0:g0 + LANES] * rstd[cb * CHUNK:(cb + 1) * CHUNK]
              * vg_ref[:, g0:g0 + LANES]).astype(BF16) for cb in range(n_chunk)], axis=1)
        yield
        s = jnp.dot(wsm_scr[g], vn, preferred_element_type=F32) + bs_ref[:, g:g + 1]
        yield
        for cb in range(n_chunk):
            r0 = cb * CHUNK
            us_scr[r0:r0 + CHUNK, g0:g0 + LANES] = (
                u_scr[r0:r0 + CHUNK, g0:g0 + LANES] * s[:, cb * CHUNK:(cb + 1) * CHUNK]).astype(BF16)

    u_projs = [in_proj(c0) for c0 in range(0, width, SGU_COL_TILE)]
    groups_per_tile = SGU_COL_TILE // LANES
    chains = [(u_projs[0], ())]
    for i, u_proj in enumerate(u_projs):
        if i + 1 < len(u_projs):
            chains.append((u_projs[i + 1], ()))
        chains.extend((gate(g), (u_proj,)) for g in range(i * groups_per_tile, (i + 1) * groups_per_tile))
    _interleave(chains, CHAINS_IN_FLIGHT)

    o_ref[...] = h + jnp.dot(us_scr[...], wout_ref[...], preferred_element_type=F32)


def _sgu_layer(layer, h, mix_gain, w_in_all, v_gain, w_s, b_s, w_out_all):
    B, S, D = h.shape
    tm = min(TOKEN_TILE, S)
    width = w_out_all.shape[1]
    return pl.pallas_call(
        functools.partial(_sgu_kernel, layer),
        grid=(B, S // tm),
        in_specs=[
            _token_spec(tm, D),
            _const_spec((1, D)),
            pl.BlockSpec(memory_space=pl.ANY),
            _const_spec((1, width)),
            _const_spec((SGU_GROUPS, CHUNK, CHUNK)),
            _const_spec((CHUNK, SGU_GROUPS)),
            pl.BlockSpec(memory_space=pl.ANY),
        ],
        out_specs=_token_spec(tm, D),
        out_shape=jax.ShapeDtypeStruct((B, S, D), F32),
        scratch_shapes=[
            pltpu.VMEM((tm, width), F32),
            pltpu.VMEM((tm, width), F32),
            pltpu.VMEM((tm, width), BF16),
            pltpu.VMEM((SGU_GROUPS, CHUNK, CHUNK), BF16),
        ] + _weight_scratch(D, 2 * width) + _weight_scratch(width, D),
        compiler_params=_compiler_params(),
        name="sgu_mixer",
    )(h, mix_gain.reshape(1, D), w_in_all, v_gain.reshape(1, width), w_s, b_s.T, w_out_all)


def _ffn_kernel(layer, h_ref, ng_ref, wup_hbm, cw_ref, cb_ref, wdown_hbm, o_ref,
                carry_scr, act_scr, perm_scr, wup_ref, wup_stage, wup_sems, wdown_ref, wdown_stage, wdown_sems):
    n_sub, tm, d_ff = act_scr.shape
    d = h_ref.shape[1]
    seg = tm // SUBLANES
    pitch = perm_scr.shape[2] // SUBLANES
    ct = FFN_COL_TILE
    n_col = d_ff // ct
    n_slab = d // LANES

    @pl.when((pl.program_id(0) == 0) & (pl.program_id(1) == 0))
    def _():
        _fetch_as_bf16(wup_hbm.at[layer], wup_ref, wup_stage, wup_sems)
        _fetch_as_bf16(wdown_hbm.at[layer], wdown_ref, wdown_stage, wdown_sems)

    @pl.when(pl.program_id(1) == 0)
    def _():
        carry_scr[...] = jnp.zeros_like(carry_scr)

    first_row = lax.broadcasted_iota(jnp.int32, (seg, 2 * ct), 0) == 0
    h_pm, hn_pm, out_pm = {}, {}, {}

    def to_phase_major(i):
        for r in range(seg):
            for c in range(n_slab):
                perm_scr[i, c, pl.ds(r, SUBLANES, stride=pitch), :] = (
                    h_ref[i * tm + r * SUBLANES:i * tm + (r + 1) * SUBLANES, c * LANES:(c + 1) * LANES])
        yield
        h_pm[i] = jnp.concatenate(
            [jnp.concatenate([perm_scr[i, c, p * pitch:p * pitch + seg, :] for c in range(n_slab)], axis=1)
             for p in range(SUBLANES)], axis=0)
        hn_pm[i] = _rms_norm(h_pm[i], ng_ref[...]).astype(BF16)

    def pair(ref, rows, c):
        return jnp.concatenate([ref[rows, c * ct:(c + 1) * ct], ref[rows, d_ff + c * ct:d_ff + (c + 1) * ct]],
                               axis=1)

    def up_tile(i, c):
        c0 = 2 * ct * c
        a = jnp.dot(hn_pm[i], pair(wup_ref, slice(None), c), preferred_element_type=F32)
        yield
        wrap1 = jnp.where(first_row, carry_scr[1:2, c0:c0 + 2 * ct], pltpu.roll(a[7 * seg:8 * seg], 1, 0))
        wrap2 = jnp.where(first_row, carry_scr[0:1, c0:c0 + 2 * ct], pltpu.roll(a[6 * seg:7 * seg], 1, 0))
        carry_scr[0:1, c0:c0 + 2 * ct] = a[7 * seg - 1:7 * seg]
        carry_scr[1:2, c0:c0 + 2 * ct] = a[8 * seg - 1:8 * seg]
        a1 = jnp.concatenate([wrap1, a[0:7 * seg]], axis=0)
        a2 = jnp.concatenate([wrap2, wrap1, a[0:6 * seg]], axis=0)
        conv = (pair(cw_ref, slice(0, 1), c) * a2 + pair(cw_ref, slice(1, 2), c) * a1
                + pair(cw_ref, slice(2, 3), c) * a + pair(cb_ref, slice(None), c))
        yield
        gate = conv[:, 0:ct]
        val = conv[:, ct:2 * ct]
        act_scr[i, :, c * ct:(c + 1) * ct] = (gate / (1.0 + jnp.exp(-gate)) * val).astype(BF16)

    def down(i):
        out_pm[i] = h_pm[i] + jnp.dot(act_scr[i], wdown_ref[...], preferred_element_type=F32)
        yield

    def to_token_order(i):
        for p in range(SUBLANES):
            for c in range(n_slab):
                perm_scr[i, c, p * pitch:p * pitch + seg, :] = (
                    out_pm[i][p * seg:(p + 1) * seg, c * LANES:(c + 1) * LANES])
        yield
        for r in range(seg):
            for c in range(n_slab):
                o_ref[i * tm + r * SUBLANES:i * tm + (r + 1) * SUBLANES, c * LANES:(c + 1) * LANES] = (
                    perm_scr[i, c, pl.ds(r, SUBLANES, stride=pitch), :])

    prep = [to_phase_major(i) for i in range(n_sub)]
    ups = [[up_tile(i, c) for c in range(n_col)] for i in range(n_sub)]
    downs = [down(i) for i in range(n_sub)]
    chains = [(prep[0], ())]
    for i in range(n_sub):
        for c in range(n_col):
            if c == n_col // 2 and i + 1 < n_sub:
                chains.append((prep[i + 1], ()))
            chains.append((ups[i][c], (prep[i],) + ((ups[i - 1][c],) if i > 0 else ())))
    for i in range(n_sub):
        chains.append((downs[i], tuple(ups[i])))
        if i > 0:
            chains.append((to_token_order(i - 1), (downs[i - 1],)))
    chains.append((to_token_order(n_sub - 1), (downs[n_sub - 1],)))
    _interleave(chains, CHAINS_IN_FLIGHT)


def _ffn_layer(layer, h, norm_gain, w_up_all, conv_w, conv_b, w_down_all):
    B, S, D = h.shape
    tm = min(TOKEN_TILE, S)
    n_sub = min(FFN_TILES_PER_STEP, S // tm)
    d_ff = w_down_all.shape[1]
    return pl.pallas_call(
        functools.partial(_ffn_kernel, layer),
        grid=(B, S // (n_sub * tm)),
        in_specs=[
            _token_spec(n_sub * tm, D),
            _const_spec((1, D)),
            pl.BlockSpec(memory_space=pl.ANY),
            _const_spec((CONV_WIDTH, 2 * d_ff)),
            _const_spec((1, 2 * d_ff)),
            pl.BlockSpec(memory_space=pl.ANY),
        ],
        out_specs=_token_spec(n_sub * tm, D),
        out_shape=jax.ShapeDtypeStruct((B, S, D), F32),
        scratch_shapes=[
            pltpu.VMEM((SUBLANES, 2 * d_ff), F32),
            pltpu.VMEM((n_sub, tm, d_ff), BF16),
            pltpu.VMEM((n_sub, D // LANES, tm + SUBLANES * SUBLANES, LANES), F32),
        ] + _weight_scratch(D, 2 * d_ff) + _weight_scratch(d_ff, D),
        compiler_params=_compiler_params(),
        name="conv_glu_ffn",
    )(h, norm_gain.reshape(1, D), w_up_all, conv_w, conv_b.reshape(1, 2 * d_ff), w_down_all)


def _band_rel_buckets():
    dist = (np.arange(CHUNK)[:, None] + CHUNK) - np.arange(2 * CHUNK)[None, :]
    n = np.maximum(dist, 0)
    max_exact = REL_BUCKETS // 2
    large = max_exact + (np.log(np.maximum(n, 1).astype(np.float32) / max_exact)
                         / math.log(REL_MAX_DIST / max_exact)
                         * (REL_BUCKETS - max_exact)).astype(np.int32)
    large = np.minimum(large, REL_BUCKETS - 1)
    bucket = np.where(n < max_exact, n, large)
    return np.where((dist >= 0) & (dist < CHUNK), bucket, -1).astype(np.int32)


def _bias_kernel(bucket_ref, rb_ref, o_ref):
    head = pl.program_id(0)
    bucket = bucket_ref[...]
    acc = jnp.full(bucket.shape, -jnp.inf, F32)
    for b in range(REL_BUCKETS):
        acc = jnp.where(bucket == b, rb_ref[b, head], acc)
    o_ref[...] = acc.T * LOG2_E


def _bias_table(rel_bias):
    bucket = jnp.asarray(_band_rel_buckets())
    return pl.pallas_call(
        _bias_kernel,
        grid=(N_HEADS,),
        in_specs=[
            pl.BlockSpec((CHUNK, 2 * CHUNK), lambda h: (0, 0)),
            pl.BlockSpec(memory_space=pltpu.SMEM),
        ],
        out_specs=pl.BlockSpec((None, 2 * CHUNK, CHUNK),
                               lambda h: (h // KV_GROUP, h % 2, (h % KV_GROUP) // 2)),
        out_shape=jax.ShapeDtypeStruct((N_KV_HEADS, 4 * CHUNK, 2 * CHUNK), F32),
        name="rel_bias_table",
    )(bucket, rel_bias)


def _head_sumsq(x, seg):
    sq = x * x
    hi = sq.astype(BF16)
    lo = (sq - hi.astype(F32)).astype(BF16)
    return (jnp.dot(hi, seg, preferred_element_type=F32)
            + jnp.dot(lo, seg, preferred_element_type=F32))


def _swa_kernel(layer, h_ref, mg_ref, wqkv_hbm, qg_ref, kg_ref, sink_ref, bias_ref, seg_ref, wo_hbm, o_ref,
                klo_scr, khi_scr, vt_scr, q_scr, att_scr,
                wqkv_ref, wqkv_stage, wqkv_sems, wo_ref, wo_stage, wo_sems):
    tm = h_ref.shape[0]
    dq = N_HEADS * HEAD_DIM
    dkv = N_KV_HEADS * HEAD_DIM
    first_tile = pl.program_id(1) == 0
    vt_rows = 3 * HEAD_DIM

    @pl.when((pl.program_id(0) == 0) & first_tile)
    def _():
        _fetch_as_bf16(wqkv_hbm.at[layer], wqkv_ref, wqkv_stage, wqkv_sems)
        _fetch_as_bf16(wo_hbm.at[layer], wo_ref, wo_stage, wo_sems)

    @pl.when(first_tile)
    def _():
        klo_scr[0:CHUNK, :] = jnp.zeros((CHUNK, klo_scr.shape[1]), BF16)
        khi_scr[0:CHUNK, :] = jnp.zeros((CHUNK, khi_scr.shape[1]), BF16)
        vt_scr[...] = jnp.zeros(vt_scr.shape, BF16)

    def store_lo_hi(lo_scr, hi_scr, x):
        zeros = jnp.zeros((tm, HEAD_DIM), F32)
        for kvh in range(N_KV_HEADS):
            x_h = x[:, kvh * HEAD_DIM:(kvh + 1) * HEAD_DIM]
            lo_scr[CHUNK:CHUNK + tm, kvh * LANES:(kvh + 1) * LANES] = (
                jnp.concatenate([x_h, zeros], axis=1).astype(BF16))
            hi_scr[CHUNK:CHUNK + tm, kvh * LANES:(kvh + 1) * LANES] = (
                jnp.concatenate([zeros, x_h], axis=1).astype(BF16))

    def store_transposed(x):
        xt = x.T
        for kvh in range(N_KV_HEADS):
            vt_scr[kvh * vt_rows + HEAD_DIM:kvh * vt_rows + 2 * HEAD_DIM, CHUNK:CHUNK + tm] = (
                xt[kvh * HEAD_DIM:(kvh + 1) * HEAD_DIM].astype(BF16))

    h = h_ref[...]
    hn = _rms_norm(h, mg_ref[...]).astype(BF16)
    seg = seg_ref[...]
    inv_hd = np.float32(1.0 / HEAD_DIM)

    def project(w0):
        xw = jnp.dot(hn, wqkv_ref[:, w0:w0 + 2 * QK_COL_TILE], preferred_element_type=F32)
        yield
        for c0 in range(w0, w0 + 2 * QK_COL_TILE, QK_COL_TILE):
            x = xw[:, c0 - w0:c0 - w0 + QK_COL_TILE]
            if c0 >= dq + dkv:
                store_transposed(x)
                yield
                continue
            r = lax.rsqrt(_head_sumsq(x, seg) * inv_hd + EPS)
            yield
            if c0 < dq:
                gain = qg_ref[:, c0:c0 + QK_COL_TILE] * (np.float32(HEAD_DIM ** -0.5) * LOG2_E)
                q_scr[:, c0:c0 + QK_COL_TILE] = (x * r * gain).astype(BF16)
            else:
                store_lo_hi(klo_scr, khi_scr, x * r * kg_ref[:, c0 - dq:c0 - dq + QK_COL_TILE])
            yield

    band_row = lax.broadcasted_iota(jnp.int32, (4 * CHUNK, 1), 0) % (2 * CHUNK)
    no_prev = jnp.where(first_tile & (band_row < CHUNK), -jnp.inf, 0.0).astype(F32)
    upper_pair = lax.broadcasted_iota(jnp.int32, (1, 2 * CHUNK), 1) >= CHUNK

    def scores(blk, kvh):
        r0 = blk * CHUNK
        c0 = kvh * KV_GROUP * HEAD_DIM
        q2 = jnp.concatenate([q_scr[r0:r0 + CHUNK, c0:c0 + LANES],
                              q_scr[r0:r0 + CHUNK, c0 + LANES:c0 + 2 * LANES]], axis=0)
        k2 = jnp.concatenate([klo_scr[r0:r0 + 2 * CHUNK, kvh * LANES:(kvh + 1) * LANES],
                              khi_scr[r0:r0 + 2 * CHUNK, kvh * LANES:(kvh + 1) * LANES]], axis=0)
        return lax.dot_general(k2, q2, (((1,), (1,)), ((), ())), preferred_element_type=F32)

    def attend(blk, kvh):
        r0 = blk * CHUNK
        c0 = kvh * KV_GROUP * HEAD_DIM
        s = scores(blk, kvh)
        yield
        s = s + bias_ref[kvh]
        if blk == 0:
            s = s + no_prev
        probs, inv_denoms = [], []
        for odd in range(2):
            t = s[odd * 2 * CHUNK:(odd + 1) * 2 * CHUNK]
            sink = LOG2_E * jnp.where(upper_pair, sink_ref[kvh * KV_GROUP + 2 + odd],
                                      sink_ref[kvh * KV_GROUP + odd])
            m = jnp.maximum(jnp.max(t, axis=0, keepdims=True), sink)
            yield
            p = jnp.exp2(t - m)
            inv_denoms.append(1.0 / (jnp.sum(p, axis=0, keepdims=True) + jnp.exp2(sink - m)))
            probs.append(p.astype(BF16))
            yield
        v0 = kvh * vt_rows
        vt2 = jnp.concatenate([vt_scr[v0 + HEAD_DIM:v0 + 3 * HEAD_DIM, r0:r0 + 2 * CHUNK],
                               vt_scr[v0:v0 + 2 * HEAD_DIM, r0:r0 + 2 * CHUNK]], axis=1)
        ot = jnp.dot(vt2, jnp.concatenate(probs, axis=0), preferred_element_type=F32)
        yield
        ot = jnp.concatenate([ot[0:HEAD_DIM] * inv_denoms[0], ot[HEAD_DIM:2 * HEAD_DIM] * inv_denoms[1]], axis=0)
        att_scr[r0:r0 + CHUNK, c0:c0 + LANES] = ot[:, 0:CHUNK].T.astype(BF16)
        att_scr[r0:r0 + CHUNK, c0 + LANES:c0 + 2 * LANES] = ot[:, CHUNK:2 * CHUNK].T.astype(BF16)

    kv_proj = project(dq)
    q_projs = [project(w0) for w0 in range(0, dq, 2 * QK_COL_TILE)]
    kvh_per_proj = 2 * QK_COL_TILE // (KV_GROUP * HEAD_DIM)
    chains = [(kv_proj, ()), (q_projs[0], ())]
    for kvh in range(N_KV_HEADS):
        chains.extend((attend(blk, kvh), (kv_proj, q_projs[kvh // kvh_per_proj])) for blk in range(tm // CHUNK))
        nxt = kvh // kvh_per_proj + 1
        if kvh % kvh_per_proj == 0 and nxt < len(q_projs):
            chains.append((q_projs[nxt], ()))
    _interleave(chains, CHAINS_IN_FLIGHT)

    klo_scr[0:CHUNK, :] = klo_scr[tm:tm + CHUNK, :]
    khi_scr[0:CHUNK, :] = khi_scr[tm:tm + CHUNK, :]
    vt_scr[:, 0:CHUNK] = vt_scr[:, tm:tm + CHUNK]

    o_ref[...] = h + jnp.dot(att_scr[...], wo_ref[...], preferred_element_type=F32)


def _swa_layer(layer, h, mix_gain, w_qkv_all, q_gain, k_gain, sinks, w_o_all, bias_table):
    B, S, D = h.shape
    tm = min(TOKEN_TILE, S)
    dq = N_HEADS * HEAD_DIM
    dkv = N_KV_HEADS * HEAD_DIM
    lane_head = np.arange(QK_COL_TILE) // HEAD_DIM
    seg = jnp.asarray(lane_head[:, None] == lane_head[None, :], BF16)
    return pl.pallas_call(
        functools.partial(_swa_kernel, layer),
        grid=(B, S // tm),
        in_specs=[
            _token_spec(tm, D),
            _const_spec((1, D)),
            pl.BlockSpec(memory_space=pl.ANY),
            _const_spec((1, dq)),
            _const_spec((1, dkv)),
            pl.BlockSpec(memory_space=pltpu.SMEM),
            _const_spec((N_KV_HEADS, 4 * CHUNK, 2 * CHUNK)),
            _const_spec((QK_COL_TILE, QK_COL_TILE)),
            pl.BlockSpec(memory_space=pl.ANY),
        ],
        out_specs=_token_spec(tm, D),
        out_shape=jax.ShapeDtypeStruct((B, S, D), F32),
        scratch_shapes=[
            pltpu.VMEM((CHUNK + tm, N_KV_HEADS * LANES), BF16),
            pltpu.VMEM((CHUNK + tm, N_KV_HEADS * LANES), BF16),
            pltpu.VMEM((N_KV_HEADS * 3 * HEAD_DIM, CHUNK + tm), BF16),
            pltpu.VMEM((tm, dq), BF16),
            pltpu.VMEM((tm, dq), BF16),
        ] + _weight_scratch(D, dq + 2 * dkv) + _weight_scratch(dq, D),
        compiler_params=_compiler_params(),
        name="swa_mixer",
    )(h, mix_gain.reshape(1, D), w_qkv_all, jnp.tile(q_gain, N_HEADS).reshape(1, dq),
      jnp.tile(k_gain, N_KV_HEADS).reshape(1, dkv), sinks, bias_table, seg, w_o_all)


def kernel(x, mix_norm, ffn_norm, sgu_w_in, sgu_v_gain, sgu_w_s, sgu_b_s, sgu_w_out, attn_w_qkv,
           attn_q_gain, attn_k_gain, attn_sinks, attn_w_o, rel_bias, ffn_w_up, ffn_conv_w, ffn_conv_b,
           ffn_w_down):
    depth = mix_norm.shape[0]
    bias_table = _bias_table(rel_bias) if depth > 1 else None
    h = x
    for i in range(depth):
        j = i // 2
        if i % 2 == 0:
            h = _sgu_layer(j, h, mix_norm[i], sgu_w_in, sgu_v_gain[j], sgu_w_s[j], sgu_b_s[j], sgu_w_out)
        else:
            h = _swa_layer(j, h, mix_norm[i], attn_w_qkv, attn_q_gain[j], attn_k_gain[j], attn_sinks[j],
                           attn_w_o, bias_table)
        h = _ffn_layer(i, h, ffn_norm[i], ffn_w_up, ffn_conv_w[i], ffn_conv_b[i], ffn_w_down)
    return h
```

```python
import functools
import math

import numpy as np
import jax
import jax.numpy as jnp
from jax import lax
from jax.experimental import pallas as pl
from jax.experimental.pallas import tpu as pltpu

F32 = jnp.float32
BF16 = jnp.bfloat16

EPS = 1e-6
LOG2_E = np.float32(1.0 / math.log(2.0))
LANES = 128
SUBLANES = 8
CHUNK = 128
HEAD_DIM = 64
N_HEADS = 16
N_KV_HEADS = 4
KV_GROUP = N_HEADS // N_KV_HEADS
SGU_GROUPS = 16
REL_BUCKETS = 32
REL_MAX_DIST = 128
CONV_WIDTH = 3

TOKEN_TILE = 512
FFN_TILES_PER_STEP = 1
FETCH_SLOTS = 3
FETCH_MAX_TILE_ROWS = 256
FETCH_MAX_TILE_BYTES = 3 * 512 * 1024
SGU_COL_TILE = 512
FFN_COL_TILE = 256
FFN_OUT_TILE = 1024
QK_COL_TILE = 256
CHAINS_IN_FLIGHT = 6
VMEM_LIMIT_BYTES = 56 * 1024 * 1024


def _rms_norm(x, gain):
    ms = jnp.mean(x * x, axis=-1, keepdims=True)
    return x * lax.rsqrt(ms + EPS) * gain


def _interleave(chains, in_flight):
    waiting = list(chains)
    running, finished = [], set()
    while waiting or running:
        if waiting and len(running) < in_flight and all(id(p) in finished for p in waiting[0][1]):
            running.append(waiting.pop(0)[0])
        for chain in list(running):
            if next(chain, "done") == "done":
                running.remove(chain)
                finished.add(id(chain))


def _fetch_as_bf16(src_hbm, dst_scr, stage, sems):
    n_slots, tile_rows, _ = stage.shape
    n_tiles = dst_scr.shape[0] // tile_rows

    def copy(k):
        return pltpu.make_async_copy(src_hbm.at[pl.ds(k * tile_rows, tile_rows), :],
                                     stage.at[k % n_slots], sems.at[k % n_slots])

    for k in range(min(n_slots - 1, n_tiles)):
        copy(k).start()
    for k in range(n_tiles):
        if k + n_slots - 1 < n_tiles:
            copy(k + n_slots - 1).start()
        copy(k).wait()
        dst_scr[k * tile_rows:(k + 1) * tile_rows, :] = stage[k % n_slots].astype(BF16)


def _weight_scratch(n_rows, n_cols):
    tile_rows = FETCH_MAX_TILE_ROWS
    while tile_rows * n_cols * 4 > FETCH_MAX_TILE_BYTES or n_rows % tile_rows:
        tile_rows //= 2
    assert tile_rows >= SUBLANES, (n_rows, n_cols)
    return [
        pltpu.VMEM((n_rows, n_cols), BF16),
        pltpu.VMEM((FETCH_SLOTS, tile_rows, n_cols), F32),
        pltpu.SemaphoreType.DMA((FETCH_SLOTS,)),
    ]


def _const_spec(shape):
    zeros = (0,) * len(shape)
    return pl.BlockSpec(shape, lambda b, j: zeros, pipeline_mode=pl.Buffered(1))


def _token_spec(tm, d):
    return pl.BlockSpec((None, tm, d), lambda b, j: (b, j, 0))


def _compiler_params():
    return pltpu.CompilerParams(
        dimension_semantics=("arbitrary", "arbitrary"),
        vmem_limit_bytes=VMEM_LIMIT_BYTES,
    )


def _sgu_kernel(layer, h_ref, mg_ref, win_hbm, vg_ref, ws_ref, bs_ref, wout_hbm, o_ref,
                u_scr, v_scr, us_scr, wsm_scr, win_ref, win_stage, win_sems, wout_ref, wout_stage, wout_sems):
    tm = h_ref.shape[0]
    width = u_scr.shape[1]

    @pl.when((pl.program_id(0) == 0) & (pl.program_id(1) == 0))
    def _():
        _fetch_as_bf16(win_hbm.at[layer], win_ref, win_stage, win_sems)
        _fetch_as_bf16(wout_hbm.at[layer], wout_ref, wout_stage, wout_sems)
        row = lax.broadcasted_iota(jnp.int32, (CHUNK, CHUNK), 0)
        col = lax.broadcasted_iota(jnp.int32, (CHUNK, CHUNK), 1)
        for g in range(SGU_GROUPS):
            wsm_scr[g] = jnp.where(row >= col, ws_ref[g], 0.0).astype(BF16)

    h = h_ref[...]
    hn = _rms_norm(h, mg_ref[...]).astype(BF16)

    v_sumsq = []

    def in_proj(c0):
        z = jnp.dot(hn, win_ref[:, c0:c0 + SGU_COL_TILE], preferred_element_type=F32)
        yield
        z = 0.5 * z * (1.0 + lax.erf(z * np.float32(math.sqrt(0.5))))
        if c0 < width:
            u_scr[:, c0:c0 + SGU_COL_TILE] = z
        else:
            v_sumsq.append(jnp.sum(z * z, axis=-1, keepdims=True))
            v_scr[:, c0 - width:c0 - width + SGU_COL_TILE] = z

    _interleave([(in_proj(c0), ()) for c0 in range(width, 2 * width, SGU_COL_TILE)], CHAINS_IN_FLIGHT)
    rstd = lax.rsqrt(sum(v_sumsq) * np.float32(1.0 / width) + EPS)

    n_chunk = tm // CHUNK

    def gate(g):
        g0 = g * LANES
        vn = jnp.concatenate(
            [(v_scr[cb * CHUNK:(cb + 1) * CHUNK, g0:g0 + LANES] * rstd[cb * CHUNK:(cb + 1) * CHUNK]
              * vg_ref[:, g0:g0 + LANES]).astype(BF16) for cb in range(n_chunk)], axis=1)
        yield
        s = jnp.dot(wsm_scr[g], vn, preferred_element_type=F32) + bs_ref[:, g:g + 1]
        yield
        for cb in range(n_chunk):
            r0 = cb * CHUNK
            us_scr[r0:r0 + CHUNK, g0:g0 + LANES] = (
                u_scr[r0:r0 + CHUNK, g0:g0 + LANES] * s[:, cb * CHUNK:(cb + 1) * CHUNK]).astype(BF16)

    u_projs = [in_proj(c0) for c0 in range(0, width, SGU_COL_TILE)]
    groups_per_tile = SGU_COL_TILE // LANES
    chains = [(u_projs[0], ())]
    for i, u_proj in enumerate(u_projs):
        if i + 1 < len(u_projs):
            chains.append((u_projs[i + 1], ()))
        chains.extend((gate(g), (u_proj,)) for g in range(i * groups_per_tile, (i + 1) * groups_per_tile))
    _interleave(chains, CHAINS_IN_FLIGHT)

    o_ref[...] = h + jnp.dot(us_scr[...], wout_ref[...], preferred_element_type=F32)


def _sgu_layer(layer, h, mix_gain, w_in_all, v_gain, w_s, b_s, w_out_all):
    B, S, D = h.shape
    tm = min(TOKEN_TILE, S)
    width = w_out_all.shape[1]
    return pl.pallas_call(
        functools.partial(_sgu_kernel, layer),
        grid=(B, S // tm),
        in_specs=[
            _token_spec(tm, D),
            _const_spec((1, D)),
            pl.BlockSpec(memory_space=pl.ANY),
            _const_spec((1, width)),
            _const_spec((SGU_GROUPS, CHUNK, CHUNK)),
            _const_spec((CHUNK, SGU_GROUPS)),
            pl.BlockSpec(memory_space=pl.ANY),
        ],
        out_specs=_token_spec(tm, D),
        out_shape=jax.ShapeDtypeStruct((B, S, D), F32),
        scratch_shapes=[
            pltpu.VMEM((tm, width), F32),
            pltpu.VMEM((tm, width), F32),
            pltpu.VMEM((tm, width), BF16),
            pltpu.VMEM((SGU_GROUPS, CHUNK, CHUNK), BF16),
        ] + _weight_scratch(D, 2 * width) + _weight_scratch(width, D),
        compiler_params=_compiler_params(),
        name="sgu_mixer",
    )(h, mix_gain.reshape(1, D), w_in_all, v_gain.reshape(1, width), w_s, b_s.T, w_out_all)


def _ffn_kernel(layer, h_ref, ng_ref, wup_hbm, cw_ref, cb_ref, wdown_hbm, o_ref,
                carry_scr, act_scr, perm_scr, wup_ref, wup_stage, wup_sems, wdown_ref, wdown_stage, wdown_sems):
    n_sub, tm, d_ff = act_scr.shape
    d = h_ref.shape[1]
    seg = tm // SUBLANES
    pitch = perm_scr.shape[2] // SUBLANES
    ct = FFN_COL_TILE
    n_col = d_ff // ct
    n_slab = d // LANES

    @pl.when((pl.program_id(0) == 0) & (pl.program_id(1) == 0))
    def _():
        _fetch_as_bf16(wup_hbm.at[layer], wup_ref, wup_stage, wup_sems)
        _fetch_as_bf16(wdown_hbm.at[layer], wdown_ref, wdown_stage, wdown_sems)

    @pl.when(pl.program_id(1) == 0)
    def _():
        carry_scr[...] = jnp.zeros_like(carry_scr)

    first_row = lax.broadcasted_iota(jnp.int32, (seg, 2 * ct), 0) == 0
    h_pm, hn_pm, out_pm = {}, {}, {}

    def to_phase_major(i):
        for r in range(seg):
            for c in range(n_slab):
                perm_scr[i, c, pl.ds(r, SUBLANES, stride=pitch), :] = (
                    h_ref[i * tm + r * SUBLANES:i * tm + (r + 1) * SUBLANES, c * LANES:(c + 1) * LANES])
        yield
        h_pm[i] = jnp.concatenate(
            [jnp.concatenate([perm_scr[i, c, p * pitch:p * pitch + seg, :] for c in range(n_slab)], axis=1)
             for p in range(SUBLANES)], axis=0)
        hn_pm[i] = _rms_norm(h_pm[i], ng_ref[...]).astype(BF16)

    def pair(ref, rows, c):
        return jnp.concatenate([ref[rows, c * ct:(c + 1) * ct], ref[rows, d_ff + c * ct:d_ff + (c + 1) * ct]],
                               axis=1)

    def up_tile(i, c):
        c0 = 2 * ct * c
        a = jnp.dot(hn_pm[i], pair(wup_ref, slice(None), c), preferred_element_type=F32)
        yield
        wrap1 = jnp.where(first_row, carry_scr[1:2, c0:c0 + 2 * ct], pltpu.roll(a[7 * seg:8 * seg], 1, 0))
        wrap2 = jnp.where(first_row, carry_scr[0:1, c0:c0 + 2 * ct], pltpu.roll(a[6 * seg:7 * seg], 1, 0))
        carry_scr[0:1, c0:c0 + 2 * ct] = a[7 * seg - 1:7 * seg]
        carry_scr[1:2, c0:c0 + 2 * ct] = a[8 * seg - 1:8 * seg]
        a1 = jnp.concatenate([wrap1, a[0:7 * seg]], axis=0)
        a2 = jnp.concatenate([wrap2, wrap1, a[0:6 * seg]], axis=0)
        conv = (pair(cw_ref, slice(0, 1), c) * a2 + pair(cw_ref, slice(1, 2), c) * a1
                + pair(cw_ref, slice(2, 3), c) * a + pair(cb_ref, slice(None), c))
        yield
        gate = conv[:, 0:ct]
        val = conv[:, ct:2 * ct]
        act_scr[i, :, c * ct:(c + 1) * ct] = (gate / (1.0 + jnp.exp(-gate)) * val).astype(BF16)

    def down(i, n0):
        n1 = n0 + FFN_OUT_TILE
        out = h_pm[i][:, n0:n1] + jnp.dot(act_scr[i], wdown_ref[:, n0:n1], preferred_element_type=F32)
        yield
        for p in range(SUBLANES):
            for c in range(n0 // LANES, n1 // LANES):
                perm_scr[i, c, p * pitch:p * pitch + seg, :] = (
                    out[p * seg:(p + 1) * seg, c * LANES - n0:(c + 1) * LANES - n0])
        yield
        for r in range(seg):
            for c in range(n0 // LANES, n1 // LANES):
                o_ref[i * tm + r * SUBLANES:i * tm + (r + 1) * SUBLANES, c * LANES:(c + 1) * LANES] = (
                    perm_scr[i, c, pl.ds(r, SUBLANES, stride=pitch), :])

    prep = [to_phase_major(i) for i in range(n_sub)]
    ups = [[up_tile(i, c) for c in range(n_col)] for i in range(n_sub)]
    chains = []
    for i in range(n_sub):
        chains.append((prep[i], ()))
        chains.extend((ups[i][c], (prep[i],) + ((ups[i - 1][c],) if i > 0 else ())) for c in range(n_col))
        chains.extend((down(i, n0), tuple(ups[i])) for n0 in range(0, d, FFN_OUT_TILE))
    _interleave(chains, CHAINS_IN_FLIGHT)


def _ffn_layer(layer, h, norm_gain, w_up_all, conv_w, conv_b, w_down_all):
    B, S, D = h.shape
    tm = min(TOKEN_TILE, S)
    n_sub = min(FFN_TILES_PER_STEP, S // tm)
    d_ff = w_down_all.shape[1]
    return pl.pallas_call(
        functools.partial(_ffn_kernel, layer),
        grid=(B, S // (n_sub * tm)),
        in_specs=[
            _token_spec(n_sub * tm, D),
            _const_spec((1, D)),
            pl.BlockSpec(memory_space=pl.ANY),
            _const_spec((CONV_WIDTH, 2 * d_ff)),
            _const_spec((1, 2 * d_ff)),
            pl.BlockSpec(memory_space=pl.ANY),
        ],
        out_specs=_token_spec(n_sub * tm, D),
        out_shape=jax.ShapeDtypeStruct((B, S, D), F32),
        scratch_shapes=[
            pltpu.VMEM((SUBLANES, 2 * d_ff), F32),
            pltpu.VMEM((n_sub, tm, d_ff), BF16),
            pltpu.VMEM((n_sub, D // LANES, tm + SUBLANES * SUBLANES, LANES), F32),
        ] + _weight_scratch(D, 2 * d_ff) + _weight_scratch(d_ff, D),
        compiler_params=_compiler_params(),
        name="conv_glu_ffn",
    )(h, norm_gain.reshape(1, D), w_up_all, conv_w, conv_b.reshape(1, 2 * d_ff), w_down_all)


def _band_rel_buckets():
    dist = (np.arange(CHUNK)[:, None] + CHUNK) - np.arange(2 * CHUNK)[None, :]
    n = np.maximum(dist, 0)
    max_exact = REL_BUCKETS // 2
    large = max_exact + (np.log(np.maximum(n, 1).astype(np.float32) / max_exact)
                         / math.log(REL_MAX_DIST / max_exact)
                         * (REL_BUCKETS - max_exact)).astype(np.int32)
    large = np.minimum(large, REL_BUCKETS - 1)
    bucket = np.where(n < max_exact, n, large)
    return np.where((dist >= 0) & (dist < CHUNK), bucket, -1).astype(np.int32)


def _bias_kernel(bucket_ref, rb_ref, o_ref):
    bucket = bucket_ref[...]
    for head in range(N_HEADS):
        acc = jnp.full(bucket.shape, -jnp.inf, F32)
        for b in range(REL_BUCKETS):
            acc = jnp.where(bucket == b, rb_ref[b, head], acc)
        kvh, pair, odd = head // KV_GROUP, (head % KV_GROUP) // 2, head % 2
        o_ref[kvh, odd * 2 * CHUNK:(odd + 1) * 2 * CHUNK, pair * CHUNK:(pair + 1) * CHUNK] = acc.T * LOG2_E


def _bias_table(rel_bias):
    bucket = jnp.asarray(_band_rel_buckets())
    return pl.pallas_call(
        _bias_kernel,
        in_specs=[
            pl.BlockSpec(memory_space=pltpu.VMEM),
            pl.BlockSpec(memory_space=pltpu.SMEM),
        ],
        out_specs=pl.BlockSpec(memory_space=pltpu.VMEM),
        out_shape=jax.ShapeDtypeStruct((N_KV_HEADS, 4 * CHUNK, 2 * CHUNK), F32),
        name="rel_bias_table",
    )(bucket, rel_bias)


def _head_sumsq(x, seg):
    return jnp.dot((x * x).astype(BF16), seg, preferred_element_type=F32)


def _swa_kernel(layer, h_ref, mg_ref, wqkv_hbm, qg_ref, kg_ref, sink_ref, bias_ref, seg_ref, wo_hbm, o_ref,
                klo_scr, khi_scr, vt_scr, q_scr, att_scr,
                wqkv_ref, wqkv_stage, wqkv_sems, wo_ref, wo_stage, wo_sems):
    tm = h_ref.shape[0]
    dq = N_HEADS * HEAD_DIM
    dkv = N_KV_HEADS * HEAD_DIM
    first_tile = pl.program_id(1) == 0
    vt_rows = 3 * HEAD_DIM

    @pl.when((pl.program_id(0) == 0) & first_tile)
    def _():
        _fetch_as_bf16(wqkv_hbm.at[layer], wqkv_ref, wqkv_stage, wqkv_sems)
        _fetch_as_bf16(wo_hbm.at[layer], wo_ref, wo_stage, wo_sems)

    @pl.when(first_tile)
    def _():
        klo_scr[0:CHUNK, :] = jnp.zeros((CHUNK, klo_scr.shape[1]), BF16)
        khi_scr[0:CHUNK, :] = jnp.zeros((CHUNK, khi_scr.shape[1]), BF16)
        vt_scr[...] = jnp.zeros(vt_scr.shape, BF16)

    def store_lo_hi(lo_scr, hi_scr, x):
        zeros = jnp.zeros((tm, HEAD_DIM), F32)
        for kvh in range(N_KV_HEADS):
            x_h = x[:, kvh * HEAD_DIM:(kvh + 1) * HEAD_DIM]
            lo_scr[CHUNK:CHUNK + tm, kvh * LANES:(kvh + 1) * LANES] = (
                jnp.concatenate([x_h, zeros], axis=1).astype(BF16))
            hi_scr[CHUNK:CHUNK + tm, kvh * LANES:(kvh + 1) * LANES] = (
                jnp.concatenate([zeros, x_h], axis=1).astype(BF16))

    def store_transposed(x):
        xt = x.T
        for kvh in range(N_KV_HEADS):
            vt_scr[kvh * vt_rows + HEAD_DIM:kvh * vt_rows + 2 * HEAD_DIM, CHUNK:CHUNK + tm] = (
                xt[kvh * HEAD_DIM:(kvh + 1) * HEAD_DIM].astype(BF16))

    h = h_ref[...]
    hn = _rms_norm(h, mg_ref[...]).astype(BF16)
    seg = seg_ref[...]
    inv_hd = np.float32(1.0 / HEAD_DIM)

    def project(w0):
        xw = jnp.dot(hn, wqkv_ref[:, w0:w0 + 2 * QK_COL_TILE], preferred_element_type=F32)
        yield
        for c0 in range(w0, w0 + 2 * QK_COL_TILE, QK_COL_TILE):
            x = xw[:, c0 - w0:c0 - w0 + QK_COL_TILE]
            if c0 >= dq + dkv:
                store_transposed(x)
                yield
                continue
            r = lax.rsqrt(_head_sumsq(x, seg) * inv_hd + EPS)
            yield
            if c0 < dq:
                gain = qg_ref[:, c0:c0 + QK_COL_TILE] * (np.float32(HEAD_DIM ** -0.5) * LOG2_E)
                q_scr[:, c0:c0 + QK_COL_TILE] = (x * r * gain).astype(BF16)
            else:
                store_lo_hi(klo_scr, khi_scr, x * r * kg_ref[:, c0 - dq:c0 - dq + QK_COL_TILE])
            yield

    band_row = lax.broadcasted_iota(jnp.int32, (4 * CHUNK, 1), 0) % (2 * CHUNK)
    no_prev = jnp.where(first_tile & (band_row < CHUNK), -jnp.inf, 0.0).astype(F32)
    upper_pair = lax.broadcasted_iota(jnp.int32, (1, 2 * CHUNK), 1) >= CHUNK

    def scores(blk, kvh):
        r0 = blk * CHUNK
        c0 = kvh * KV_GROUP * HEAD_DIM
        q2 = jnp.concatenate([q_scr[r0:r0 + CHUNK, c0:c0 + LANES],
                              q_scr[r0:r0 + CHUNK, c0 + LANES:c0 + 2 * LANES]], axis=0)
        k2 = jnp.concatenate([klo_scr[r0:r0 + 2 * CHUNK, kvh * LANES:(kvh + 1) * LANES],
                              khi_scr[r0:r0 + 2 * CHUNK, kvh * LANES:(kvh + 1) * LANES]], axis=0)
        return lax.dot_general(k2, q2, (((1,), (1,)), ((), ())), preferred_element_type=F32)

    def attend(blk, kvh):
        r0 = blk * CHUNK
        c0 = kvh * KV_GROUP * HEAD_DIM
        s = scores(blk, kvh)
        yield
        s = s + bias_ref[kvh]
        if blk == 0:
            s = s + no_prev
        probs, inv_denoms = [], []
        for odd in range(2):
            t = s[odd * 2 * CHUNK:(odd + 1) * 2 * CHUNK]
            sink = LOG2_E * jnp.where(upper_pair, sink_ref[kvh * KV_GROUP + 2 + odd],
                                      sink_ref[kvh * KV_GROUP + odd])
            m = jnp.maximum(jnp.max(t, axis=0, keepdims=True), sink)
            yield
            p = jnp.exp2(t - m)
            inv_denoms.append(1.0 / (jnp.sum(p, axis=0, keepdims=True) + jnp.exp2(sink - m)))
            probs.append(p.astype(BF16))
            yield
        v0 = kvh * vt_rows
        vt2 = jnp.concatenate([vt_scr[v0 + HEAD_DIM:v0 + 3 * HEAD_DIM, r0:r0 + 2 * CHUNK],
                               vt_scr[v0:v0 + 2 * HEAD_DIM, r0:r0 + 2 * CHUNK]], axis=1)
        ot = jnp.dot(vt2, jnp.concatenate(probs, axis=0), preferred_element_type=F32)
        yield
        ot = jnp.concatenate([ot[0:HEAD_DIM] * inv_denoms[0], ot[HEAD_DIM:2 * HEAD_DIM] * inv_denoms[1]], axis=0)
        att_scr[r0:r0 + CHUNK, c0:c0 + LANES] = ot[:, 0:CHUNK].T.astype(BF16)
        att_scr[r0:r0 + CHUNK, c0 + LANES:c0 + 2 * LANES] = ot[:, CHUNK:2 * CHUNK].T.astype(BF16)

    kv_proj = project(dq)
    q_projs = [project(w0) for w0 in range(0, dq, 2 * QK_COL_TILE)]
    kvh_per_proj = 2 * QK_COL_TILE // (KV_GROUP * HEAD_DIM)
    chains = [(kv_proj, ()), (q_projs[0], ())]
    for kvh in range(N_KV_HEADS):
        chains.extend((attend(blk, kvh), (kv_proj, q_projs[kvh // kvh_per_proj])) for blk in range(tm // CHUNK))
        nxt = kvh // kvh_per_proj + 1
        if kvh % kvh_per_proj == 0 and nxt < len(q_projs):
            chains.append((q_projs[nxt], ()))
    _interleave(chains, CHAINS_IN_FLIGHT)

    klo_scr[0:CHUNK, :] = klo_scr[tm:tm + CHUNK, :]
    khi_scr[0:CHUNK, :] = khi_scr[tm:tm + CHUNK, :]
    vt_scr[:, 0:CHUNK] = vt_scr[:, tm:tm + CHUNK]

    o_ref[...] = h + jnp.dot(att_scr[...], wo_ref[...], preferred_element_type=F32)


def _swa_layer(layer, h, mix_gain, w_qkv_all, q_gain, k_gain, sinks, w_o_all, bias_table):
    B, S, D = h.shape
    tm = min(TOKEN_TILE, S)
    dq = N_HEADS * HEAD_DIM
    dkv = N_KV_HEADS * HEAD_DIM
    lane_head = np.arange(QK_COL_TILE) // HEAD_DIM
    seg = jnp.asarray(lane_head[:, None] == lane_head[None, :], BF16)
    return pl.pallas_call(
        functools.partial(_swa_kernel, layer),
        grid=(B, S // tm),
        in_specs=[
            _token_spec(tm, D),
            _const_spec((1, D)),
            pl.BlockSpec(memory_space=pl.ANY),
            _const_spec((1, dq)),
            _const_spec((1, dkv)),
            pl.BlockSpec(memory_space=pltpu.SMEM),
            _const_spec((N_KV_HEADS, 4 * CHUNK, 2 * CHUNK)),
            _const_spec((QK_COL_TILE, QK_COL_TILE)),
            pl.BlockSpec(memory_space=pl.ANY),
        ],
        out_specs=_token_spec(tm, D),
        out_shape=jax.ShapeDtypeStruct((B, S, D), F32),
        scratch_shapes=[
            pltpu.VMEM((CHUNK + tm, N_KV_HEADS * LANES), BF16),
            pltpu.VMEM((CHUNK + tm, N_KV_HEADS * LANES), BF16),
            pltpu.VMEM((N_KV_HEADS * 3 * HEAD_DIM, CHUNK + tm), BF16),
            pltpu.VMEM((tm, dq), BF16),
            pltpu.VMEM((tm, dq), BF16),
        ] + _weight_scratch(D, dq + 2 * dkv) + _weight_scratch(dq, D),
        compiler_params=_compiler_params(),
        name="swa_mixer",
    )(h, mix_gain.reshape(1, D), w_qkv_all, jnp.tile(q_gain, N_HEADS).reshape(1, dq),
      jnp.tile(k_gain, N_KV_HEADS).reshape(1, dkv), sinks, bias_table, seg, w_o_all)


def kernel(x, mix_norm, ffn_norm, sgu_w_in, sgu_v_gain, sgu_w_s, sgu_b_s, sgu_w_out, attn_w_qkv,
           attn_q_gain, attn_k_gain, attn_sinks, attn_w_o, rel_bias, ffn_w_up, ffn_conv_w, ffn_conv_b,
           ffn_w_down):
    depth = mix_norm.shape[0]
    bias_table = _bias_table(rel_bias) if depth > 1 else None
    h = x
    for i in range(depth):
        j = i // 2
        if i % 2 == 0:
            h = _sgu_layer(j, h, mix_norm[i], sgu_w_in, sgu_v_gain[j], sgu_w_s[j], sgu_b_s[j], sgu_w_out)
        else:
            h = _swa_layer(j, h, mix_norm[i], attn_w_qkv, attn_q_gain[j], attn_k_gain[j], attn_sinks[j],
                           attn_w_o, bias_table)
        h = _ffn_layer(i, h, ffn_norm[i], ffn_w_up, ffn_conv_w[i], ffn_conv_b[i], ffn_w_down)
    return h
```

```python
import functools
import math

import numpy as np
import jax
import jax.numpy as jnp
from jax import lax
from jax.experimental import pallas as pl
from jax.experimental.pallas import tpu as pltpu

F32 = jnp.float32
BF16 = jnp.bfloat16

EPS = 1e-6
LOG2_E = np.float32(1.0 / math.log(2.0))
LANES = 128
SUBLANES = 8
CHUNK = 128
HEAD_DIM = 64
N_HEADS = 16
N_KV_HEADS = 4
KV_GROUP = N_HEADS // N_KV_HEADS
SGU_GROUPS = 16
REL_BUCKETS = 32
REL_MAX_DIST = 128
CONV_WIDTH = 3

TOKEN_TILE = 512
FFN_TILES_PER_STEP = 1
FETCH_SLOTS = 3
FETCH_MAX_TILE_ROWS = 256
FETCH_MAX_TILE_BYTES = 3 * 512 * 1024
SGU_COL_TILE = 512
FFN_COL_TILE = 256
FFN_OUT_TILE = 1024
QK_COL_TILE = 256
CHAINS_IN_FLIGHT = 6
VMEM_LIMIT_BYTES = 56 * 1024 * 1024


def _rms_norm(x, gain):
    ms = jnp.mean(x * x, axis=-1, keepdims=True)
    return x * lax.rsqrt(ms + EPS) * gain


def _interleave(chains, in_flight):
    waiting = list(chains)
    running, finished = [], set()
    while waiting or running:
        if waiting and len(running) < in_flight and all(id(p) in finished for p in waiting[0][1]):
            running.append(waiting.pop(0)[0])
        for chain in list(running):
            if next(chain, "done") == "done":
                running.remove(chain)
                finished.add(id(chain))


def _fetch_as_bf16(src_hbm, dst_scr, stage, sems):
    n_slots, tile_rows, _ = stage.shape
    n_tiles = dst_scr.shape[0] // tile_rows

    def copy(k):
        return pltpu.make_async_copy(src_hbm.at[pl.ds(k * tile_rows, tile_rows), :],
                                     stage.at[k % n_slots], sems.at[k % n_slots])

    for k in range(min(n_slots - 1, n_tiles)):
        copy(k).start()
    for k in range(n_tiles):
        if k + n_slots - 1 < n_tiles:
            copy(k + n_slots - 1).start()
        copy(k).wait()
        dst_scr[k * tile_rows:(k + 1) * tile_rows, :] = stage[k % n_slots].astype(BF16)


def _weight_scratch(n_rows, n_cols):
    tile_rows = FETCH_MAX_TILE_ROWS
    while tile_rows * n_cols * 4 > FETCH_MAX_TILE_BYTES or n_rows % tile_rows:
        tile_rows //= 2
    assert tile_rows >= SUBLANES, (n_rows, n_cols)
    return [
        pltpu.VMEM((n_rows, n_cols), BF16),
        pltpu.VMEM((FETCH_SLOTS, tile_rows, n_cols), F32),
        pltpu.SemaphoreType.DMA((FETCH_SLOTS,)),
    ]


def _tile_loop(h_hbm, o_hbm, hbuf, obuf, in_sems, out_sems, tile_fn):
    n_batch, seq, _ = h_hbm.shape
    tm = hbuf.shape[1]
    tiles_per_seq = seq // tm
    n_tiles = n_batch * tiles_per_seq

    def hbm_tile(ref, t):
        return ref.at[t // tiles_per_seq, pl.ds(pl.multiple_of((t % tiles_per_seq) * tm, tm), tm), :]

    def in_copy(t, slot):
        return pltpu.make_async_copy(hbm_tile(h_hbm, t), hbuf.at[slot], in_sems.at[slot])

    def out_copy(t, slot):
        return pltpu.make_async_copy(obuf.at[slot], hbm_tile(o_hbm, t), out_sems.at[slot])

    in_copy(0, 0).start()

    def step(t, carry):
        slot = t % 2
        in_copy(t, slot).wait()

        @pl.when(t + 1 < n_tiles)
        def _():
            in_copy(t + 1, 1 - slot).start()

        @pl.when(t >= 2)
        def _():
            out_copy(t - 2, slot).wait()

        tile_fn(t % tiles_per_seq == 0, hbuf.at[slot], obuf.at[slot])
        out_copy(t, slot).start()
        return carry

    lax.fori_loop(0, n_tiles, step, 0)
    for t in range(max(n_tiles - 2, 0), n_tiles):
        out_copy(t, t % 2).wait()


def _tile_buffers(tm, d):
    return [
        pltpu.VMEM((2, tm, d), F32),
        pltpu.VMEM((2, tm, d), F32),
        pltpu.SemaphoreType.DMA((2,)),
        pltpu.SemaphoreType.DMA((2,)),
    ]


_VMEM_SPEC = pl.BlockSpec(memory_space=pltpu.VMEM)
_HBM_SPEC = pl.BlockSpec(memory_space=pl.ANY)


def _compiler_params():
    return pltpu.CompilerParams(vmem_limit_bytes=VMEM_LIMIT_BYTES)


def _sgu_kernel(layer, h_hbm, mg_ref, win_hbm, vg_ref, ws_ref, bs_ref, wout_hbm, o_hbm,
                u_scr, v_scr, us_scr, wsm_scr, win_ref, win_stage, win_sems, wout_ref, wout_stage, wout_sems,
                hbuf, obuf, in_sems, out_sems):
    _fetch_as_bf16(win_hbm.at[layer], win_ref, win_stage, win_sems)
    _fetch_as_bf16(wout_hbm.at[layer], wout_ref, wout_stage, wout_sems)
    row = lax.broadcasted_iota(jnp.int32, (CHUNK, CHUNK), 0)
    col = lax.broadcasted_iota(jnp.int32, (CHUNK, CHUNK), 1)
    for g in range(SGU_GROUPS):
        wsm_scr[g] = jnp.where(row >= col, ws_ref[g], 0.0).astype(BF16)

    def tile(first_tile, h_ref, o_ref):
        del first_tile
        _sgu_tile(h_ref, mg_ref, vg_ref, bs_ref, o_ref, u_scr, v_scr, us_scr, wsm_scr, win_ref, wout_ref)

    _tile_loop(h_hbm, o_hbm, hbuf, obuf, in_sems, out_sems, tile)


def _sgu_tile(h_ref, mg_ref, vg_ref, bs_ref, o_ref, u_scr, v_scr, us_scr, wsm_scr, win_ref, wout_ref):
    tm = h_ref.shape[0]
    width = u_scr.shape[1]
    h = h_ref[...]
    hn = _rms_norm(h, mg_ref[...]).astype(BF16)

    v_sumsq = []

    def in_proj(c0):
        z = jnp.dot(hn, win_ref[:, c0:c0 + SGU_COL_TILE], preferred_element_type=F32)
        yield
        z = 0.5 * z * (1.0 + lax.erf(z * np.float32(math.sqrt(0.5))))
        if c0 < width:
            u_scr[:, c0:c0 + SGU_COL_TILE] = z
        else:
            v_sumsq.append(jnp.sum(z * z, axis=-1, keepdims=True))
            v_scr[:, c0 - width:c0 - width + SGU_COL_TILE] = z

    _interleave([(in_proj(c0), ()) for c0 in range(width, 2 * width, SGU_COL_TILE)], CHAINS_IN_FLIGHT)
    rstd = lax.rsqrt(sum(v_sumsq) * np.float32(1.0 / width) + EPS)

    n_chunk = tm // CHUNK

    def gate(g):
        g0 = g * LANES
        vn = jnp.concatenate(
            [(v_scr[cb * CHUNK:(cb + 1) * CHUNK, g0:g0 + LANES] * rstd[cb * CHUNK:(cb + 1) * CHUNK]
              * vg_ref[:, g0:g0 + LANES]).astype(BF16) for cb in range(n_chunk)], axis=1)
        yield
        s = jnp.dot(wsm_scr[g], vn, preferred_element_type=F32) + bs_ref[:, g:g + 1]
        yield
        for cb in range(n_chunk):
            r0 = cb * CHUNK
            us_scr[r0:r0 + CHUNK, g0:g0 + LANES] = (
                u_scr[r0:r0 + CHUNK, g0:g0 + LANES] * s[:, cb * CHUNK:(cb + 1) * CHUNK]).astype(BF16)

    u_projs = [in_proj(c0) for c0 in range(0, width, SGU_COL_TILE)]
    groups_per_tile = SGU_COL_TILE // LANES
    chains = [(u_projs[0], ())]
    for i, u_proj in enumerate(u_projs):
        if i + 1 < len(u_projs):
            chains.append((u_projs[i + 1], ()))
        chains.extend((gate(g), (u_proj,)) for g in range(i * groups_per_tile, (i + 1) * groups_per_tile))
    _interleave(chains, CHAINS_IN_FLIGHT)

    o_ref[...] = h + jnp.dot(us_scr[...], wout_ref[...], preferred_element_type=F32)


def _sgu_layer(layer, h, mix_gain, w_in_all, v_gain, w_s, b_s, w_out_all):
    B, S, D = h.shape
    tm = min(TOKEN_TILE, S)
    width = w_out_all.shape[1]
    return pl.pallas_call(
        functools.partial(_sgu_kernel, layer),
        in_specs=[_HBM_SPEC, _VMEM_SPEC, _HBM_SPEC, _VMEM_SPEC, _VMEM_SPEC, _VMEM_SPEC, _HBM_SPEC],
        out_specs=_HBM_SPEC,
        out_shape=jax.ShapeDtypeStruct((B, S, D), F32),
        scratch_shapes=[
            pltpu.VMEM((tm, width), F32),
            pltpu.VMEM((tm, width), F32),
            pltpu.VMEM((tm, width), BF16),
            pltpu.VMEM((SGU_GROUPS, CHUNK, CHUNK), BF16),
        ] + _weight_scratch(D, 2 * width) + _weight_scratch(width, D) + _tile_buffers(tm, D),
        compiler_params=_compiler_params(),
        name="sgu_mixer",
    )(h, mix_gain.reshape(1, D), w_in_all, v_gain.reshape(1, width), w_s, b_s.T, w_out_all)


def _ffn_kernel(layer, h_hbm, ng_ref, wup_hbm, cw_ref, cb_ref, wdown_hbm, o_hbm,
                carry_scr, act_scr, perm_scr, wup_ref, wup_stage, wup_sems, wdown_ref, wdown_stage, wdown_sems,
                hbuf, obuf, in_sems, out_sems):
    _fetch_as_bf16(wup_hbm.at[layer], wup_ref, wup_stage, wup_sems)
    _fetch_as_bf16(wdown_hbm.at[layer], wdown_ref, wdown_stage, wdown_sems)

    def tile(first_tile, h_ref, o_ref):
        _ffn_tile(first_tile, h_ref, ng_ref, cw_ref, cb_ref, o_ref, carry_scr, act_scr, perm_scr, wup_ref, wdown_ref)

    _tile_loop(h_hbm, o_hbm, hbuf, obuf, in_sems, out_sems, tile)


def _ffn_tile(first_tile, h_ref, ng_ref, cw_ref, cb_ref, o_ref, carry_scr, act_scr, perm_scr, wup_ref, wdown_ref):
    n_sub, tm, d_ff = act_scr.shape
    d = h_ref.shape[1]
    seg = tm // SUBLANES
    pitch = perm_scr.shape[2] // SUBLANES
    ct = FFN_COL_TILE
    n_col = d_ff // ct
    n_slab = d // LANES

    @pl.when(first_tile)
    def _():
        carry_scr[...] = jnp.zeros_like(carry_scr)

    first_row = lax.broadcasted_iota(jnp.int32, (seg, 2 * ct), 0) == 0
    h_pm, hn_pm, out_pm = {}, {}, {}

    def to_phase_major(i):
        for r in range(seg):
            for c in range(n_slab):
                perm_scr[i, c, pl.ds(r, SUBLANES, stride=pitch), :] = (
                    h_ref[i * tm + r * SUBLANES:i * tm + (r + 1) * SUBLANES, c * LANES:(c + 1) * LANES])
        yield
        h_pm[i] = jnp.concatenate(
            [jnp.concatenate([perm_scr[i, c, p * pitch:p * pitch + seg, :] for c in range(n_slab)], axis=1)
             for p in range(SUBLANES)], axis=0)
        hn_pm[i] = _rms_norm(h_pm[i], ng_ref[...]).astype(BF16)

    def pair(ref, rows, c):
        return jnp.concatenate([ref[rows, c * ct:(c + 1) * ct], ref[rows, d_ff + c * ct:d_ff + (c + 1) * ct]],
                               axis=1)

    def up_tile(i, c):
        c0 = 2 * ct * c
        a = jnp.dot(hn_pm[i], pair(wup_ref, slice(None), c), preferred_element_type=F32)
        yield
        wrap1 = jnp.where(first_row, carry_scr[1:2, c0:c0 + 2 * ct], pltpu.roll(a[7 * seg:8 * seg], 1, 0))
        wrap2 = jnp.where(first_row, carry_scr[0:1, c0:c0 + 2 * ct], pltpu.roll(a[6 * seg:7 * seg], 1, 0))
        carry_scr[0:1, c0:c0 + 2 * ct] = a[7 * seg - 1:7 * seg]
        carry_scr[1:2, c0:c0 + 2 * ct] = a[8 * seg - 1:8 * seg]
        a1 = jnp.concatenate([wrap1, a[0:7 * seg]], axis=0)
        a2 = jnp.concatenate([wrap2, wrap1, a[0:6 * seg]], axis=0)
        conv = (pair(cw_ref, slice(0, 1), c) * a2 + pair(cw_ref, slice(1, 2), c) * a1
                + pair(cw_ref, slice(2, 3), c) * a + pair(cb_ref, slice(None), c))
        yield
        gate = conv[:, 0:ct]
        val = conv[:, ct:2 * ct]
        act_scr[i, :, c * ct:(c + 1) * ct] = (gate / (1.0 + jnp.exp(-gate)) * val).astype(BF16)

    def down(i, n0):
        n1 = n0 + FFN_OUT_TILE
        out = h_pm[i][:, n0:n1] + jnp.dot(act_scr[i], wdown_ref[:, n0:n1], preferred_element_type=F32)
        yield
        for p in range(SUBLANES):
            for c in range(n0 // LANES, n1 // LANES):
                perm_scr[i, c, p * pitch:p * pitch + seg, :] = (
                    out[p * seg:(p + 1) * seg, c * LANES - n0:(c + 1) * LANES - n0])
        yield
        for r in range(seg):
            for c in range(n0 // LANES, n1 // LANES):
                o_ref[i * tm + r * SUBLANES:i * tm + (r + 1) * SUBLANES, c * LANES:(c + 1) * LANES] = (
                    perm_scr[i, c, pl.ds(r, SUBLANES, stride=pitch), :])

    prep = [to_phase_major(i) for i in range(n_sub)]
    ups = [[up_tile(i, c) for c in range(n_col)] for i in range(n_sub)]
    chains = []
    for i in range(n_sub):
        chains.append((prep[i], ()))
        chains.extend((ups[i][c], (prep[i],) + ((ups[i - 1][c],) if i > 0 else ())) for c in range(n_col))
        chains.extend((down(i, n0), tuple(ups[i])) for n0 in range(0, d, FFN_OUT_TILE))
    _interleave(chains, CHAINS_IN_FLIGHT)


def _ffn_layer(layer, h, norm_gain, w_up_all, conv_w, conv_b, w_down_all):
    B, S, D = h.shape
    tm = min(TOKEN_TILE, S)
    n_sub = min(FFN_TILES_PER_STEP, S // tm)
    d_ff = w_down_all.shape[1]
    return pl.pallas_call(
        functools.partial(_ffn_kernel, layer),
        in_specs=[_HBM_SPEC, _VMEM_SPEC, _HBM_SPEC, _VMEM_SPEC, _VMEM_SPEC, _HBM_SPEC],
        out_specs=_HBM_SPEC,
        out_shape=jax.ShapeDtypeStruct((B, S, D), F32),
        scratch_shapes=[
            pltpu.VMEM((SUBLANES, 2 * d_ff), F32),
            pltpu.VMEM((n_sub, tm, d_ff), BF16),
            pltpu.VMEM((n_sub, D // LANES, tm + SUBLANES * SUBLANES, LANES), F32),
        ] + _weight_scratch(D, 2 * d_ff) + _weight_scratch(d_ff, D) + _tile_buffers(n_sub * tm, D),
        compiler_params=_compiler_params(),
        name="conv_glu_ffn",
    )(h, norm_gain.reshape(1, D), w_up_all, conv_w, conv_b.reshape(1, 2 * d_ff), w_down_all)


def _band_rel_buckets():
    dist = (np.arange(CHUNK)[:, None] + CHUNK) - np.arange(2 * CHUNK)[None, :]
    n = np.maximum(dist, 0)
    max_exact = REL_BUCKETS // 2
    large = max_exact + (np.log(np.maximum(n, 1).astype(np.float32) / max_exact)
                         / math.log(REL_MAX_DIST / max_exact)
                         * (REL_BUCKETS - max_exact)).astype(np.int32)
    large = np.minimum(large, REL_BUCKETS - 1)
    bucket = np.where(n < max_exact, n, large)
    return np.where((dist >= 0) & (dist < CHUNK), bucket, -1).astype(np.int32)


def _bias_kernel(bucket_ref, rb_ref, o_ref):
    bucket = bucket_ref[...]
    for head in range(N_HEADS):
        acc = jnp.full(bucket.shape, -jnp.inf, F32)
        for b in range(REL_BUCKETS):
            acc = jnp.where(bucket == b, rb_ref[b, head], acc)
        kvh, pair, odd = head // KV_GROUP, (head % KV_GROUP) // 2, head % 2
        o_ref[kvh, odd * 2 * CHUNK:(odd + 1) * 2 * CHUNK, pair * CHUNK:(pair + 1) * CHUNK] = acc.T * LOG2_E


def _bias_table(rel_bias):
    bucket = jnp.asarray(_band_rel_buckets())
    return pl.pallas_call(
        _bias_kernel,
        in_specs=[
            pl.BlockSpec(memory_space=pltpu.VMEM),
            pl.BlockSpec(memory_space=pltpu.SMEM),
        ],
        out_specs=pl.BlockSpec(memory_space=pltpu.VMEM),
        out_shape=jax.ShapeDtypeStruct((N_KV_HEADS, 4 * CHUNK, 2 * CHUNK), F32),
        name="rel_bias_table",
    )(bucket, rel_bias)


def _head_sumsq(x, seg):
    return jnp.dot((x * x).astype(BF16), seg, preferred_element_type=F32)


def _swa_kernel(layer, h_hbm, mg_ref, wqkv_hbm, qg_ref, kg_ref, sink_ref, bias_ref, seg_ref, wo_hbm, o_hbm,
                klo_scr, khi_scr, vt_scr, q_scr, att_scr,
                wqkv_ref, wqkv_stage, wqkv_sems, wo_ref, wo_stage, wo_sems,
                hbuf, obuf, in_sems, out_sems):
    _fetch_as_bf16(wqkv_hbm.at[layer], wqkv_ref, wqkv_stage, wqkv_sems)
    _fetch_as_bf16(wo_hbm.at[layer], wo_ref, wo_stage, wo_sems)

    def tile(first_tile, h_ref, o_ref):
        _swa_tile(first_tile, h_ref, mg_ref, qg_ref, kg_ref, sink_ref, bias_ref, seg_ref, o_ref,
                  klo_scr, khi_scr, vt_scr, q_scr, att_scr, wqkv_ref, wo_ref)

    _tile_loop(h_hbm, o_hbm, hbuf, obuf, in_sems, out_sems, tile)


def _swa_tile(first_tile, h_ref, mg_ref, qg_ref, kg_ref, sink_ref, bias_ref, seg_ref, o_ref,
              klo_scr, khi_scr, vt_scr, q_scr, att_scr, wqkv_ref, wo_ref):
    tm = h_ref.shape[0]
    dq = N_HEADS * HEAD_DIM
    dkv = N_KV_HEADS * HEAD_DIM
    vt_rows = 3 * HEAD_DIM

    @pl.when(first_tile)
    def _():
        klo_scr[0:CHUNK, :] = jnp.zeros((CHUNK, klo_scr.shape[1]), BF16)
        khi_scr[0:CHUNK, :] = jnp.zeros((CHUNK, khi_scr.shape[1]), BF16)
        vt_scr[...] = jnp.zeros(vt_scr.shape, BF16)

    def store_lo_hi(lo_scr, hi_scr, x):
        zeros = jnp.zeros((tm, HEAD_DIM), F32)
        for kvh in range(N_KV_HEADS):
            x_h = x[:, kvh * HEAD_DIM:(kvh + 1) * HEAD_DIM]
            lo_scr[CHUNK:CHUNK + tm, kvh * LANES:(kvh + 1) * LANES] = (
                jnp.concatenate([x_h, zeros], axis=1).astype(BF16))
            hi_scr[CHUNK:CHUNK + tm, kvh * LANES:(kvh + 1) * LANES] = (
                jnp.concatenate([zeros, x_h], axis=1).astype(BF16))

    def store_transposed(x):
        xt = x.T
        for kvh in range(N_KV_HEADS):
            vt_scr[kvh * vt_rows + HEAD_DIM:kvh * vt_rows + 2 * HEAD_DIM, CHUNK:CHUNK + tm] = (
                xt[kvh * HEAD_DIM:(kvh + 1) * HEAD_DIM].astype(BF16))

    h = h_ref[...]
    hn = _rms_norm(h, mg_ref[...]).astype(BF16)
    seg = seg_ref[...]
    inv_hd = np.float32(1.0 / HEAD_DIM)

    def project(w0):
        xw = jnp.dot(hn, wqkv_ref[:, w0:w0 + 2 * QK_COL_TILE], preferred_element_type=F32)
        yield
        for c0 in range(w0, w0 + 2 * QK_COL_TILE, QK_COL_TILE):
            x = xw[:, c0 - w0:c0 - w0 + QK_COL_TILE]
            if c0 >= dq + dkv:
                store_transposed(x)
                yield
                continue
            r = lax.rsqrt(_head_sumsq(x, seg) * inv_hd + EPS)
            yield
            if c0 < dq:
                gain = qg_ref[:, c0:c0 + QK_COL_TILE] * (np.float32(HEAD_DIM ** -0.5) * LOG2_E)
                q_scr[:, c0:c0 + QK_COL_TILE] = (x * r * gain).astype(BF16)
            else:
                store_lo_hi(klo_scr, khi_scr, x * r * kg_ref[:, c0 - dq:c0 - dq + QK_COL_TILE])
            yield

    band_row = lax.broadcasted_iota(jnp.int32, (4 * CHUNK, 1), 0) % (2 * CHUNK)
    no_prev = jnp.where(first_tile & (band_row < CHUNK), -jnp.inf, 0.0).astype(F32)
    upper_pair = lax.broadcasted_iota(jnp.int32, (1, 2 * CHUNK), 1) >= CHUNK

    def scores(blk, kvh):
        r0 = blk * CHUNK
        c0 = kvh * KV_GROUP * HEAD_DIM
        q2 = jnp.concatenate([q_scr[r0:r0 + CHUNK, c0:c0 + LANES],
                              q_scr[r0:r0 + CHUNK, c0 + LANES:c0 + 2 * LANES]], axis=0)
        k2 = jnp.concatenate([klo_scr[r0:r0 + 2 * CHUNK, kvh * LANES:(kvh + 1) * LANES],
                              khi_scr[r0:r0 + 2 * CHUNK, kvh * LANES:(kvh + 1) * LANES]], axis=0)
        return lax.dot_general(k2, q2, (((1,), (1,)), ((), ())), preferred_element_type=F32)

    def attend(blk, kvh):
        r0 = blk * CHUNK
        c0 = kvh * KV_GROUP * HEAD_DIM
        s = scores(blk, kvh)
        yield
        s = s + bias_ref[kvh]
        if blk == 0:
            s = s + no_prev
        probs, inv_denoms = [], []
        for odd in range(2):
            t = s[odd * 2 * CHUNK:(odd + 1) * 2 * CHUNK]
            sink = LOG2_E * jnp.where(upper_pair, sink_ref[kvh * KV_GROUP + 2 + odd],
                                      sink_ref[kvh * KV_GROUP + odd])
            m = jnp.maximum(jnp.max(t, axis=0, keepdims=True), sink)
            yield
            p = jnp.exp2(t - m)
            inv_denoms.append(1.0 / (jnp.sum(p, axis=0, keepdims=True) + jnp.exp2(sink - m)))
            probs.append(p.astype(BF16))
            yield
        v0 = kvh * vt_rows
        vt2 = jnp.concatenate([vt_scr[v0 + HEAD_DIM:v0 + 3 * HEAD_DIM, r0:r0 + 2 * CHUNK],
                               vt_scr[v0:v0 + 2 * HEAD_DIM, r0:r0 + 2 * CHUNK]], axis=1)
        ot = jnp.dot(vt2, jnp.concatenate(probs, axis=0), preferred_element_type=F32)
        yield
        ot = jnp.concatenate([ot[0:HEAD_DIM] * inv_denoms[0], ot[HEAD_DIM:2 * HEAD_DIM] * inv_denoms[1]], axis=0)
        att_scr[r0:r0 + CHUNK, c0:c0 + LANES] = ot[:, 0:CHUNK].T.astype(BF16)
        att_scr[r0:r0 + CHUNK, c0 + LANES:c0 + 2 * LANES] = ot[:, CHUNK:2 * CHUNK].T.astype(BF16)

    kv_proj = project(dq)
    q_projs = [project(w0) for w0 in range(0, dq, 2 * QK_COL_TILE)]
    kvh_per_proj = 2 * QK_COL_TILE // (KV_GROUP * HEAD_DIM)
    chains = [(kv_proj, ()), (q_projs[0], ())]
    for kvh in range(N_KV_HEADS):
        chains.extend((attend(blk, kvh), (kv_proj, q_projs[kvh // kvh_per_proj])) for blk in range(tm // CHUNK))
        nxt = kvh // kvh_per_proj + 1
        if kvh % kvh_per_proj == 0 and nxt < len(q_projs):
            chains.append((q_projs[nxt], ()))
    _interleave(chains, CHAINS_IN_FLIGHT)

    klo_scr[0:CHUNK, :] = klo_scr[tm:tm + CHUNK, :]
    khi_scr[0:CHUNK, :] = khi_scr[tm:tm + CHUNK, :]
    vt_scr[:, 0:CHUNK] = vt_scr[:, tm:tm + CHUNK]

    o_ref[...] = h + jnp.dot(att_scr[...], wo_ref[...], preferred_element_type=F32)


def _swa_layer(layer, h, mix_gain, w_qkv_all, q_gain, k_gain, sinks, w_o_all, bias_table):
    B, S, D = h.shape
    tm = min(TOKEN_TILE, S)
    dq = N_HEADS * HEAD_DIM
    dkv = N_KV_HEADS * HEAD_DIM
    lane_head = np.arange(QK_COL_TILE) // HEAD_DIM
    seg = jnp.asarray(lane_head[:, None] == lane_head[None, :], BF16)
    return pl.pallas_call(
        functools.partial(_swa_kernel, layer),
        in_specs=[_HBM_SPEC, _VMEM_SPEC, _HBM_SPEC, _VMEM_SPEC, _VMEM_SPEC,
                  pl.BlockSpec(memory_space=pltpu.SMEM), _VMEM_SPEC, _VMEM_SPEC, _HBM_SPEC],
        out_specs=_HBM_SPEC,
        out_shape=jax.ShapeDtypeStruct((B, S, D), F32),
        scratch_shapes=[
            pltpu.VMEM((CHUNK + tm, N_KV_HEADS * LANES), BF16),
            pltpu.VMEM((CHUNK + tm, N_KV_HEADS * LANES), BF16),
            pltpu.VMEM((N_KV_HEADS * 3 * HEAD_DIM, CHUNK + tm), BF16),
            pltpu.VMEM((tm, dq), BF16),
            pltpu.VMEM((tm, dq), BF16),
        ] + _weight_scratch(D, dq + 2 * dkv) + _weight_scratch(dq, D) + _tile_buffers(tm, D),
        compiler_params=_compiler_params(),
        name="swa_mixer",
    )(h, mix_gain.reshape(1, D), w_qkv_all, jnp.tile(q_gain, N_HEADS).reshape(1, dq),
      jnp.tile(k_gain, N_KV_HEADS).reshape(1, dkv), sinks, bias_table, seg, w_o_all)


def kernel(x, mix_norm, ffn_norm, sgu_w_in, sgu_v_gain, sgu_w_s, sgu_b_s, sgu_w_out, attn_w_qkv,
           attn_q_gain, attn_k_gain, attn_sinks, attn_w_o, rel_bias, ffn_w_up, ffn_conv_w, ffn_conv_b,
           ffn_w_down):
    depth = mix_norm.shape[0]
    bias_table = _bias_table(rel_bias) if depth > 1 else None
    h = x
    for i in range(depth):
        j = i // 2
        if i % 2 == 0:
            h = _sgu_layer(j, h, mix_norm[i], sgu_w_in, sgu_v_gain[j], sgu_w_s[j], sgu_b_s[j], sgu_w_out)
        else:
            h = _swa_layer(j, h, mix_norm[i], attn_w_qkv, attn_q_gain[j], attn_k_gain[j], attn_sinks[j],
                           attn_w_o, bias_table)
        h = _ffn_layer(i, h, ffn_norm[i], ffn_w_up, ffn_conv_w[i], ffn_conv_b[i], ffn_w_down)
    return h
```

```python
import functools
import math

import numpy as np
import jax
import jax.numpy as jnp
from jax import lax
from jax.experimental import pallas as pl
from jax.experimental.pallas import tpu as pltpu

F32 = jnp.float32
BF16 = jnp.bfloat16

EPS = 1e-6
LOG2_E = np.float32(1.0 / math.log(2.0))
LANES = 128
SUBLANES = 8
CHUNK = 128
HEAD_DIM = 64
N_HEADS = 16
N_KV_HEADS = 4
KV_GROUP = N_HEADS // N_KV_HEADS
SGU_GROUPS = 16
REL_BUCKETS = 32
REL_MAX_DIST = 128
CONV_WIDTH = 3

TOKEN_TILE = 512
FFN_TILES_PER_STEP = 1
FETCH_SLOTS = 3
FETCH_MAX_TILE_ROWS = 256
FETCH_MAX_TILE_BYTES = 3 * 512 * 1024
SGU_COL_TILE = 512
FFN_COL_TILE = 256
FFN_OUT_TILE = 1024
QK_COL_TILE = 256
CHAINS_IN_FLIGHT = 6
VMEM_LIMIT_BYTES = 56 * 1024 * 1024


def _rms_norm(x, gain):
    ms = jnp.mean(x * x, axis=-1, keepdims=True)
    return x * lax.rsqrt(ms + EPS) * gain


def _interleave(chains, in_flight):
    waiting = list(chains)
    running, finished = [], set()
    while waiting or running:
        if waiting and len(running) < in_flight and all(id(p) in finished for p in waiting[0][1]):
            running.append(waiting.pop(0)[0])
        for chain in list(running):
            if next(chain, "done") == "done":
                running.remove(chain)
                finished.add(id(chain))


def _fetch_as_bf16(src_hbm, dst_scr, stage, sems):
    n_slots, tile_rows, _ = stage.shape
    n_tiles = dst_scr.shape[0] // tile_rows

    def copy(k):
        return pltpu.make_async_copy(src_hbm.at[pl.ds(k * tile_rows, tile_rows), :],
                                     stage.at[k % n_slots], sems.at[k % n_slots])

    for k in range(min(n_slots - 1, n_tiles)):
        copy(k).start()
    for k in range(n_tiles):
        if k + n_slots - 1 < n_tiles:
            copy(k + n_slots - 1).start()
        copy(k).wait()
        dst_scr[k * tile_rows:(k + 1) * tile_rows, :] = stage[k % n_slots].astype(BF16)


def _weight_scratch(n_rows, n_cols):
    tile_rows = FETCH_MAX_TILE_ROWS
    while tile_rows * n_cols * 4 > FETCH_MAX_TILE_BYTES or n_rows % tile_rows:
        tile_rows //= 2
    assert tile_rows >= SUBLANES, (n_rows, n_cols)
    return [
        pltpu.VMEM((n_rows, n_cols), BF16),
        pltpu.VMEM((FETCH_SLOTS, tile_rows, n_cols), F32),
        pltpu.SemaphoreType.DMA((FETCH_SLOTS,)),
    ]


def _tile_loop(h_hbm, o_hbm, hbuf, obuf, in_sems, out_sems, tile_fn):
    n_batch, seq, _ = h_hbm.shape
    tm = hbuf.shape[1]
    tiles_per_seq = seq // tm
    n_tiles = n_batch * tiles_per_seq

    def hbm_tile(ref, t):
        return ref.at[t // tiles_per_seq, pl.ds(pl.multiple_of((t % tiles_per_seq) * tm, tm), tm), :]

    def in_copy(t, slot):
        return pltpu.make_async_copy(hbm_tile(h_hbm, t), hbuf.at[slot], in_sems.at[slot])

    def out_copy(t, slot):
        return pltpu.make_async_copy(obuf.at[slot], hbm_tile(o_hbm, t), out_sems.at[slot])

    in_copy(0, 0).start()

    def step(t, carry):
        slot = t % 2
        in_copy(t, slot).wait()

        @pl.when(t + 1 < n_tiles)
        def _():
            in_copy(t + 1, 1 - slot).start()

        @pl.when(t >= 2)
        def _():
            out_copy(t - 2, slot).wait()

        tile_fn(t % tiles_per_seq == 0, hbuf.at[slot], obuf.at[slot])
        out_copy(t, slot).start()
        return carry

    lax.fori_loop(0, n_tiles, step, 0)
    for t in range(max(n_tiles - 2, 0), n_tiles):
        out_copy(t, t % 2).wait()


def _tile_buffers(tm, d):
    return [
        pltpu.VMEM((2, tm, d), F32),
        pltpu.VMEM((2, tm, d), F32),
        pltpu.SemaphoreType.DMA((2,)),
        pltpu.SemaphoreType.DMA((2,)),
    ]


_VMEM_SPEC = pl.BlockSpec(memory_space=pltpu.VMEM)
_HBM_SPEC = pl.BlockSpec(memory_space=pl.ANY)


def _compiler_params():
    return pltpu.CompilerParams(vmem_limit_bytes=VMEM_LIMIT_BYTES)


def _sgu_kernel(layer, h_hbm, mg_ref, win_hbm, vg_ref, ws_ref, bs_ref, wout_hbm, o_hbm,
                u_scr, v_scr, us_scr, wsm_scr, win_ref, win_stage, win_sems, wout_ref, wout_stage, wout_sems,
                hbuf, obuf, in_sems, out_sems):
    _fetch_as_bf16(win_hbm.at[layer], win_ref, win_stage, win_sems)
    _fetch_as_bf16(wout_hbm.at[layer], wout_ref, wout_stage, wout_sems)
    row = lax.broadcasted_iota(jnp.int32, (CHUNK, CHUNK), 0)
    col = lax.broadcasted_iota(jnp.int32, (CHUNK, CHUNK), 1)
    for g in range(SGU_GROUPS):
        wsm_scr[g] = jnp.where(row >= col, ws_ref[g], 0.0).astype(BF16)

    def tile(first_tile, h_ref, o_ref):
        del first_tile
        _sgu_tile(h_ref, mg_ref, vg_ref, bs_ref, o_ref, u_scr, v_scr, us_scr, wsm_scr, win_ref, wout_ref)

    _tile_loop(h_hbm, o_hbm, hbuf, obuf, in_sems, out_sems, tile)


def _sgu_tile(h_ref, mg_ref, vg_ref, bs_ref, o_ref, u_scr, v_scr, us_scr, wsm_scr, win_ref, wout_ref):
    tm = h_ref.shape[0]
    width = wout_ref.shape[0]
    h = h_ref[...]
    hn = _rms_norm(h, mg_ref[...]).astype(BF16)

    v_sumsq = []

    def in_proj(c0):
        z = jnp.dot(hn, win_ref[:, c0:c0 + SGU_COL_TILE], preferred_element_type=F32)
        yield
        z = 0.5 * z * (1.0 + lax.erf(z * np.float32(math.sqrt(0.5))))
        if c0 < width:
            u_scr[:, c0:c0 + SGU_COL_TILE] = z
        else:
            v_sumsq.append(jnp.sum(z * z, axis=-1, keepdims=True))
            v_scr[:, c0 - width:c0 - width + SGU_COL_TILE] = z

    _interleave([(in_proj(c0), ()) for c0 in range(width, 2 * width, SGU_COL_TILE)], CHAINS_IN_FLIGHT)
    rstd = lax.rsqrt(sum(v_sumsq) * np.float32(1.0 / width) + EPS)

    n_chunk = tm // CHUNK

    def gate(g):
        g0 = g * LANES
        vn = jnp.concatenate(
            [(v_scr[cb * CHUNK:(cb + 1) * CHUNK, g0:g0 + LANES] * rstd[cb * CHUNK:(cb + 1) * CHUNK]
              * vg_ref[:, g0:g0 + LANES]).astype(BF16) for cb in range(n_chunk)], axis=1)
        yield
        s = jnp.dot(wsm_scr[g], vn, preferred_element_type=F32) + bs_ref[:, g:g + 1]
        yield
        for cb in range(n_chunk):
            r0 = cb * CHUNK
            us_scr[r0:r0 + CHUNK, g0:g0 + LANES] = (
                u_scr[r0:r0 + CHUNK, g0:g0 + LANES] * s[:, cb * CHUNK:(cb + 1) * CHUNK]).astype(BF16)

    u_projs = [in_proj(c0) for c0 in range(0, width, SGU_COL_TILE)]
    groups_per_tile = SGU_COL_TILE // LANES
    chains = [(u_projs[0], ())]
    for i, u_proj in enumerate(u_projs):
        if i + 1 < len(u_projs):
            chains.append((u_projs[i + 1], ()))
        chains.extend((gate(g), (u_proj,)) for g in range(i * groups_per_tile, (i + 1) * groups_per_tile))
    _interleave(chains, CHAINS_IN_FLIGHT)

    o_ref[...] = h + jnp.dot(us_scr[:, 0:width], wout_ref[...], preferred_element_type=F32)


def _sgu_layer(layer, h, mix_gain, w_in_all, v_gain, w_s, b_s, w_out_all):
    B, S, D = h.shape
    tm = min(TOKEN_TILE, S)
    width = w_out_all.shape[1]
    return pl.pallas_call(
        functools.partial(_sgu_kernel, layer),
        in_specs=[_HBM_SPEC, _VMEM_SPEC, _HBM_SPEC, _VMEM_SPEC, _VMEM_SPEC, _VMEM_SPEC, _HBM_SPEC],
        out_specs=_HBM_SPEC,
        out_shape=jax.ShapeDtypeStruct((B, S, D), F32),
        scratch_shapes=[
            pltpu.VMEM((tm, width + LANES), F32),
            pltpu.VMEM((tm, width + LANES), F32),
            pltpu.VMEM((tm, width + LANES), BF16),
            pltpu.VMEM((SGU_GROUPS, CHUNK, CHUNK), BF16),
        ] + _weight_scratch(D, 2 * width) + _weight_scratch(width, D) + _tile_buffers(tm, D),
        compiler_params=_compiler_params(),
        name="sgu_mixer",
    )(h, mix_gain.reshape(1, D), w_in_all, v_gain.reshape(1, width), w_s, b_s.T, w_out_all)


def _ffn_kernel(layer, h_hbm, ng_ref, wup_hbm, cw_ref, cb_ref, wdown_hbm, o_hbm,
                carry_scr, act_scr, perm_scr, wup_ref, wup_stage, wup_sems, wdown_ref, wdown_stage, wdown_sems,
                hbuf, obuf, in_sems, out_sems):
    _fetch_as_bf16(wup_hbm.at[layer], wup_ref, wup_stage, wup_sems)
    _fetch_as_bf16(wdown_hbm.at[layer], wdown_ref, wdown_stage, wdown_sems)

    def tile(first_tile, h_ref, o_ref):
        _ffn_tile(first_tile, h_ref, ng_ref, cw_ref, cb_ref, o_ref, carry_scr, act_scr, perm_scr, wup_ref, wdown_ref)

    _tile_loop(h_hbm, o_hbm, hbuf, obuf, in_sems, out_sems, tile)


def _ffn_tile(first_tile, h_ref, ng_ref, cw_ref, cb_ref, o_ref, carry_scr, act_scr, perm_scr, wup_ref, wdown_ref):
    n_sub, tm, d_ff = act_scr.shape
    d = h_ref.shape[1]
    seg = tm // SUBLANES
    pitch = perm_scr.shape[2] // SUBLANES
    ct = FFN_COL_TILE
    n_col = d_ff // ct
    n_slab = d // LANES

    @pl.when(first_tile)
    def _():
        carry_scr[...] = jnp.zeros_like(carry_scr)

    first_row = lax.broadcasted_iota(jnp.int32, (seg, 2 * ct), 0) == 0
    h_pm, hn_pm, out_pm = {}, {}, {}

    def to_phase_major(i):
        for r in range(seg):
            for c in range(n_slab):
                perm_scr[i, c, pl.ds(r, SUBLANES, stride=pitch), :] = (
                    h_ref[i * tm + r * SUBLANES:i * tm + (r + 1) * SUBLANES, c * LANES:(c + 1) * LANES])
        yield
        h_pm[i] = jnp.concatenate(
            [jnp.concatenate([perm_scr[i, c, p * pitch:p * pitch + seg, :] for c in range(n_slab)], axis=1)
             for p in range(SUBLANES)], axis=0)
        hn_pm[i] = _rms_norm(h_pm[i], ng_ref[...]).astype(BF16)

    def pair(ref, rows, c):
        return jnp.concatenate([ref[rows, c * ct:(c + 1) * ct], ref[rows, d_ff + c * ct:d_ff + (c + 1) * ct]],
                               axis=1)

    def up_tile(i, c):
        c0 = 2 * ct * c
        a = jnp.dot(hn_pm[i], pair(wup_ref, slice(None), c), preferred_element_type=F32)
        yield
        wrap1 = jnp.where(first_row, carry_scr[1:2, c0:c0 + 2 * ct], pltpu.roll(a[7 * seg:8 * seg], 1, 0))
        wrap2 = jnp.where(first_row, carry_scr[0:1, c0:c0 + 2 * ct], pltpu.roll(a[6 * seg:7 * seg], 1, 0))
        carry_scr[0:1, c0:c0 + 2 * ct] = a[7 * seg - 1:7 * seg]
        carry_scr[1:2, c0:c0 + 2 * ct] = a[8 * seg - 1:8 * seg]
        a1 = jnp.concatenate([wrap1, a[0:7 * seg]], axis=0)
        a2 = jnp.concatenate([wrap2, wrap1, a[0:6 * seg]], axis=0)
        conv = (pair(cw_ref, slice(0, 1), c) * a2 + pair(cw_ref, slice(1, 2), c) * a1
                + pair(cw_ref, slice(2, 3), c) * a + pair(cb_ref, slice(None), c))
        yield
        gate = conv[:, 0:ct]
        val = conv[:, ct:2 * ct]
        act_scr[i, :, c * ct:(c + 1) * ct] = (gate / (1.0 + jnp.exp(-gate)) * val).astype(BF16)

    def down(i, n0):
        n1 = n0 + FFN_OUT_TILE
        out = h_pm[i][:, n0:n1] + jnp.dot(act_scr[i], wdown_ref[:, n0:n1], preferred_element_type=F32)
        yield
        for p in range(SUBLANES):
            for c in range(n0 // LANES, n1 // LANES):
                perm_scr[i, c, p * pitch:p * pitch + seg, :] = (
                    out[p * seg:(p + 1) * seg, c * LANES - n0:(c + 1) * LANES - n0])
        yield
        for r in range(seg):
            for c in range(n0 // LANES, n1 // LANES):
                o_ref[i * tm + r * SUBLANES:i * tm + (r + 1) * SUBLANES, c * LANES:(c + 1) * LANES] = (
                    perm_scr[i, c, pl.ds(r, SUBLANES, stride=pitch), :])

    prep = [to_phase_major(i) for i in range(n_sub)]
    ups = [[up_tile(i, c) for c in range(n_col)] for i in range(n_sub)]
    chains = []
    for i in range(n_sub):
        chains.append((prep[i], ()))
        chains.extend((ups[i][c], (prep[i],) + ((ups[i - 1][c],) if i > 0 else ())) for c in range(n_col))
        chains.extend((down(i, n0), tuple(ups[i])) for n0 in range(0, d, FFN_OUT_TILE))
    _interleave(chains, CHAINS_IN_FLIGHT)


def _ffn_layer(layer, h, norm_gain, w_up_all, conv_w, conv_b, w_down_all):
    B, S, D = h.shape
    tm = min(TOKEN_TILE, S)
    n_sub = min(FFN_TILES_PER_STEP, S // tm)
    d_ff = w_down_all.shape[1]
    return pl.pallas_call(
        functools.partial(_ffn_kernel, layer),
        in_specs=[_HBM_SPEC, _VMEM_SPEC, _HBM_SPEC, _VMEM_SPEC, _VMEM_SPEC, _HBM_SPEC],
        out_specs=_HBM_SPEC,
        out_shape=jax.ShapeDtypeStruct((B, S, D), F32),
        scratch_shapes=[
            pltpu.VMEM((SUBLANES, 2 * d_ff), F32),
            pltpu.VMEM((n_sub, tm, d_ff), BF16),
            pltpu.VMEM((n_sub, D // LANES, tm + SUBLANES * SUBLANES, LANES), F32),
        ] + _weight_scratch(D, 2 * d_ff) + _weight_scratch(d_ff, D) + _tile_buffers(n_sub * tm, D),
        compiler_params=_compiler_params(),
        name="conv_glu_ffn",
    )(h, norm_gain.reshape(1, D), w_up_all, conv_w, conv_b.reshape(1, 2 * d_ff), w_down_all)


def _band_rel_buckets():
    dist = (np.arange(CHUNK)[:, None] + CHUNK) - np.arange(2 * CHUNK)[None, :]
    n = np.maximum(dist, 0)
    max_exact = REL_BUCKETS // 2
    large = max_exact + (np.log(np.maximum(n, 1).astype(np.float32) / max_exact)
                         / math.log(REL_MAX_DIST / max_exact)
                         * (REL_BUCKETS - max_exact)).astype(np.int32)
    large = np.minimum(large, REL_BUCKETS - 1)
    bucket = np.where(n < max_exact, n, large)
    return np.where((dist >= 0) & (dist < CHUNK), bucket, -1).astype(np.int32)


def _bias_kernel(bucket_ref, rb_ref, o_ref):
    bucket = bucket_ref[...]
    for head in range(N_HEADS):
        acc = jnp.full(bucket.shape, -jnp.inf, F32)
        for b in range(REL_BUCKETS):
            acc = jnp.where(bucket == b, rb_ref[b, head], acc)
        kvh, pair, odd = head // KV_GROUP, (head % KV_GROUP) // 2, head % 2
        o_ref[kvh, odd * 2 * CHUNK:(odd + 1) * 2 * CHUNK, pair * CHUNK:(pair + 1) * CHUNK] = acc.T * LOG2_E


def _bias_table(rel_bias):
    bucket = jnp.asarray(_band_rel_buckets())
    return pl.pallas_call(
        _bias_kernel,
        in_specs=[
            pl.BlockSpec(memory_space=pltpu.VMEM),
            pl.BlockSpec(memory_space=pltpu.SMEM),
        ],
        out_specs=pl.BlockSpec(memory_space=pltpu.VMEM),
        out_shape=jax.ShapeDtypeStruct((N_KV_HEADS, 4 * CHUNK, 2 * CHUNK), F32),
        name="rel_bias_table",
    )(bucket, rel_bias)


def _head_sumsq(x, seg):
    return jnp.dot((x * x).astype(BF16), seg, preferred_element_type=F32)


def _swa_kernel(layer, h_hbm, mg_ref, wqkv_hbm, qg_ref, kg_ref, sink_ref, bias_ref, seg_ref, wo_hbm, o_hbm,
                klo_scr, khi_scr, vt_scr, q_scr, att_scr,
                wqkv_ref, wqkv_stage, wqkv_sems, wo_ref, wo_stage, wo_sems,
                hbuf, obuf, in_sems, out_sems):
    _fetch_as_bf16(wqkv_hbm.at[layer], wqkv_ref, wqkv_stage, wqkv_sems)
    _fetch_as_bf16(wo_hbm.at[layer], wo_ref, wo_stage, wo_sems)

    def tile(first_tile, h_ref, o_ref):
        _swa_tile(first_tile, h_ref, mg_ref, qg_ref, kg_ref, sink_ref, bias_ref, seg_ref, o_ref,
                  klo_scr, khi_scr, vt_scr, q_scr, att_scr, wqkv_ref, wo_ref)

    _tile_loop(h_hbm, o_hbm, hbuf, obuf, in_sems, out_sems, tile)


def _swa_tile(first_tile, h_ref, mg_ref, qg_ref, kg_ref, sink_ref, bias_ref, seg_ref, o_ref,
              klo_scr, khi_scr, vt_scr, q_scr, att_scr, wqkv_ref, wo_ref):
    tm = h_ref.shape[0]
    dq = N_HEADS * HEAD_DIM
    dkv = N_KV_HEADS * HEAD_DIM
    vt_rows = 3 * HEAD_DIM

    @pl.when(first_tile)
    def _():
        klo_scr[0:CHUNK, :] = jnp.zeros((CHUNK, klo_scr.shape[1]), BF16)
        khi_scr[0:CHUNK, :] = jnp.zeros((CHUNK, khi_scr.shape[1]), BF16)
        vt_scr[...] = jnp.zeros(vt_scr.shape, BF16)

    def store_lo_hi(lo_scr, hi_scr, x):
        zeros = jnp.zeros((tm, HEAD_DIM), F32)
        for kvh in range(N_KV_HEADS):
            x_h = x[:, kvh * HEAD_DIM:(kvh + 1) * HEAD_DIM]
            lo_scr[CHUNK:CHUNK + tm, kvh * LANES:(kvh + 1) * LANES] = (
                jnp.concatenate([x_h, zeros], axis=1).astype(BF16))
            hi_scr[CHUNK:CHUNK + tm, kvh * LANES:(kvh + 1) * LANES] = (
                jnp.concatenate([zeros, x_h], axis=1).astype(BF16))

    def store_transposed(x):
        xt = x.T
        for kvh in range(N_KV_HEADS):
            vt_scr[kvh * vt_rows + HEAD_DIM:kvh * vt_rows + 2 * HEAD_DIM, CHUNK:CHUNK + tm] = (
                xt[kvh * HEAD_DIM:(kvh + 1) * HEAD_DIM].astype(BF16))

    h = h_ref[...]
    hn = _rms_norm(h, mg_ref[...]).astype(BF16)
    seg = seg_ref[...]
    inv_hd = np.float32(1.0 / HEAD_DIM)

    def project(w0):
        xw = jnp.dot(hn, wqkv_ref[:, w0:w0 + 2 * QK_COL_TILE], preferred_element_type=F32)
        yield
        for c0 in range(w0, w0 + 2 * QK_COL_TILE, QK_COL_TILE):
            x = xw[:, c0 - w0:c0 - w0 + QK_COL_TILE]
            if c0 >= dq + dkv:
                store_transposed(x)
                yield
                continue
            r = lax.rsqrt(_head_sumsq(x, seg) * inv_hd + EPS)
            yield
            if c0 < dq:
                gain = qg_ref[:, c0:c0 + QK_COL_TILE] * (np.float32(HEAD_DIM ** -0.5) * LOG2_E)
                q_scr[:, c0:c0 + QK_COL_TILE] = (x * r * gain).astype(BF16)
            else:
                store_lo_hi(klo_scr, khi_scr, x * r * kg_ref[:, c0 - dq:c0 - dq + QK_COL_TILE])
            yield

    band_row = lax.broadcasted_iota(jnp.int32, (4 * CHUNK, 1), 0) % (2 * CHUNK)
    no_prev = jnp.where(first_tile & (band_row < CHUNK), -jnp.inf, 0.0).astype(F32)
    upper_pair = lax.broadcasted_iota(jnp.int32, (1, 2 * CHUNK), 1) >= CHUNK

    def scores(blk, kvh):
        r0 = blk * CHUNK
        c0 = kvh * KV_GROUP * HEAD_DIM
        q2 = jnp.concatenate([q_scr[r0:r0 + CHUNK, c0:c0 + LANES],
                              q_scr[r0:r0 + CHUNK, c0 + LANES:c0 + 2 * LANES]], axis=0)
        k2 = jnp.concatenate([klo_scr[r0:r0 + 2 * CHUNK, kvh * LANES:(kvh + 1) * LANES],
                              khi_scr[r0:r0 + 2 * CHUNK, kvh * LANES:(kvh + 1) * LANES]], axis=0)
        return lax.dot_general(k2, q2, (((1,), (1,)), ((), ())), preferred_element_type=F32)

    def attend(blk, kvh):
        r0 = blk * CHUNK
        c0 = kvh * KV_GROUP * HEAD_DIM
        s = scores(blk, kvh)
        yield
        s = s + bias_ref[kvh]
        if blk == 0:
            s = s + no_prev
        probs, inv_denoms = [], []
        for odd in range(2):
            t = s[odd * 2 * CHUNK:(odd + 1) * 2 * CHUNK]
            sink = LOG2_E * jnp.where(upper_pair, sink_ref[kvh * KV_GROUP + 2 + odd],
                                      sink_ref[kvh * KV_GROUP + odd])
            m = jnp.maximum(jnp.max(t, axis=0, keepdims=True), sink)
            yield
            p = jnp.exp2(t - m)
            inv_denoms.append(1.0 / (jnp.sum(p, axis=0, keepdims=True) + jnp.exp2(sink - m)))
            probs.append(p.astype(BF16))
            yield
        v0 = kvh * vt_rows
        vt2 = jnp.concatenate([vt_scr[v0 + HEAD_DIM:v0 + 3 * HEAD_DIM, r0:r0 + 2 * CHUNK],
                               vt_scr[v0:v0 + 2 * HEAD_DIM, r0:r0 + 2 * CHUNK]], axis=1)
        ot = jnp.dot(vt2, jnp.concatenate(probs, axis=0), preferred_element_type=F32)
        yield
        ot = jnp.concatenate([ot[0:HEAD_DIM] * inv_denoms[0], ot[HEAD_DIM:2 * HEAD_DIM] * inv_denoms[1]], axis=0)
        att_scr[r0:r0 + CHUNK, c0:c0 + LANES] = ot[:, 0:CHUNK].T.astype(BF16)
        att_scr[r0:r0 + CHUNK, c0 + LANES:c0 + 2 * LANES] = ot[:, CHUNK:2 * CHUNK].T.astype(BF16)

    kv_proj = project(dq)
    q_projs = [project(w0) for w0 in range(0, dq, 2 * QK_COL_TILE)]
    kvh_per_proj = 2 * QK_COL_TILE // (KV_GROUP * HEAD_DIM)
    chains = [(kv_proj, ()), (q_projs[0], ())]
    for kvh in range(N_KV_HEADS):
        chains.extend((attend(blk, kvh), (kv_proj, q_projs[kvh // kvh_per_proj])) for blk in range(tm // CHUNK))
        nxt = kvh // kvh_per_proj + 1
        if kvh % kvh_per_proj == 0 and nxt < len(q_projs):
            chains.append((q_projs[nxt], ()))
    _interleave(chains, CHAINS_IN_FLIGHT)

    kv_cols = N_KV_HEADS * LANES
    klo_scr[0:CHUNK, 0:kv_cols] = klo_scr[tm:tm + CHUNK, 0:kv_cols]
    khi_scr[0:CHUNK, 0:kv_cols] = khi_scr[tm:tm + CHUNK, 0:kv_cols]
    vt_scr[:, 0:CHUNK] = vt_scr[:, tm:tm + CHUNK]

    o_ref[...] = h + jnp.dot(att_scr[:, 0:dq], wo_ref[...], preferred_element_type=F32)


def _swa_layer(layer, h, mix_gain, w_qkv_all, q_gain, k_gain, sinks, w_o_all, bias_table):
    B, S, D = h.shape
    tm = min(TOKEN_TILE, S)
    dq = N_HEADS * HEAD_DIM
    dkv = N_KV_HEADS * HEAD_DIM
    lane_head = np.arange(QK_COL_TILE) // HEAD_DIM
    seg = jnp.asarray(lane_head[:, None] == lane_head[None, :], BF16)
    return pl.pallas_call(
        functools.partial(_swa_kernel, layer),
        in_specs=[_HBM_SPEC, _VMEM_SPEC, _HBM_SPEC, _VMEM_SPEC, _VMEM_SPEC,
                  pl.BlockSpec(memory_space=pltpu.SMEM), _VMEM_SPEC, _VMEM_SPEC, _HBM_SPEC],
        out_specs=_HBM_SPEC,
        out_shape=jax.ShapeDtypeStruct((B, S, D), F32),
        scratch_shapes=[
            pltpu.VMEM((CHUNK + tm, (N_KV_HEADS + 1) * LANES), BF16),
            pltpu.VMEM((CHUNK + tm, (N_KV_HEADS + 1) * LANES), BF16),
            pltpu.VMEM((N_KV_HEADS * 3 * HEAD_DIM, CHUNK + tm), BF16),
            pltpu.VMEM((tm, dq + LANES), BF16),
            pltpu.VMEM((tm, dq + LANES), BF16),
        ] + _weight_scratch(D, dq + 2 * dkv) + _weight_scratch(dq, D) + _tile_buffers(tm, D),
        compiler_params=_compiler_params(),
        name="swa_mixer",
    )(h, mix_gain.reshape(1, D), w_qkv_all, jnp.tile(q_gain, N_HEADS).reshape(1, dq),
      jnp.tile(k_gain, N_KV_HEADS).reshape(1, dkv), sinks, bias_table, seg, w_o_all)


def kernel(x, mix_norm, ffn_norm, sgu_w_in, sgu_v_gain, sgu_w_s, sgu_b_s, sgu_w_out, attn_w_qkv,
           attn_q_gain, attn_k_gain, attn_sinks, attn_w_o, rel_bias, ffn_w_up, ffn_conv_w, ffn_conv_b,
           ffn_w_down):
    depth = mix_norm.shape[0]
    bias_table = _bias_table(rel_bias) if depth > 1 else None
    h = x
    for i in range(depth):
        j = i // 2
        if i % 2 == 0:
            h = _sgu_layer(j, h, mix_norm[i], sgu_w_in, sgu_v_gain[j], sgu_w_s[j], sgu_b_s[j], sgu_w_out)
        else:
            h = _swa_layer(j, h, mix_norm[i], attn_w_qkv, attn_q_gain[j], attn_k_gain[j], attn_sinks[j],
                           attn_w_o, bias_table)
        h = _ffn_layer(i, h, ffn_norm[i], ffn_w_up, ffn_conv_w[i], ffn_conv_b[i], ffn_w_down)
    return h
```

```python
import functools
import math

import numpy as np
import jax
import jax.numpy as jnp
from jax import lax
from jax.experimental import pallas as pl
from jax.experimental.pallas import tpu as pltpu

F32 = jnp.float32
BF16 = jnp.bfloat16

EPS = 1e-6
LOG2_E = np.float32(1.0 / math.log(2.0))
LANES = 128
SUBLANES = 8
CHUNK = 128
HEAD_DIM = 64
N_HEADS = 16
N_KV_HEADS = 4
KV_GROUP = N_HEADS // N_KV_HEADS
SGU_GROUPS = 16
REL_BUCKETS = 32
REL_MAX_DIST = 128
CONV_WIDTH = 3

TOKEN_TILE = 512
FFN_TILES_PER_STEP = 1
FETCH_SLOTS = 3
FETCH_MAX_TILE_ROWS = 256
FETCH_MAX_TILE_BYTES = 3 * 512 * 1024
SGU_COL_TILE = 512
FFN_COL_TILE = 256
FFN_OUT_TILE = 1024
QK_COL_TILE = 256
CHAINS_IN_FLIGHT = 6
VMEM_LIMIT_BYTES = 56 * 1024 * 1024


def _rms_norm(x, gain):
    ms = jnp.mean(x * x, axis=-1, keepdims=True)
    return x * lax.rsqrt(ms + EPS) * gain


def _interleave(chains, in_flight):
    waiting = list(chains)
    running, finished = [], set()
    while waiting or running:
        if waiting and len(running) < in_flight and all(id(p) in finished for p in waiting[0][1]):
            running.append(waiting.pop(0)[0])
        for chain in list(running):
            if next(chain, "done") == "done":
                running.remove(chain)
                finished.add(id(chain))


def _fetch_as_bf16(src_hbm, dst_scr, stage, sems):
    n_slots, tile_rows, _ = stage.shape
    n_tiles = dst_scr.shape[0] // tile_rows

    def copy(k):
        return pltpu.make_async_copy(src_hbm.at[pl.ds(k * tile_rows, tile_rows), :],
                                     stage.at[k % n_slots], sems.at[k % n_slots])

    for k in range(min(n_slots - 1, n_tiles)):
        copy(k).start()
    for k in range(n_tiles):
        if k + n_slots - 1 < n_tiles:
            copy(k + n_slots - 1).start()
        copy(k).wait()
        dst_scr[k * tile_rows:(k + 1) * tile_rows, :] = stage[k % n_slots].astype(BF16)


def _weight_scratch(n_rows, n_cols):
    tile_rows = FETCH_MAX_TILE_ROWS
    while tile_rows * n_cols * 4 > FETCH_MAX_TILE_BYTES or n_rows % tile_rows:
        tile_rows //= 2
    assert tile_rows >= SUBLANES, (n_rows, n_cols)
    return [
        pltpu.VMEM((n_rows, n_cols), BF16),
        pltpu.VMEM((FETCH_SLOTS, tile_rows, n_cols), F32),
        pltpu.SemaphoreType.DMA((FETCH_SLOTS,)),
    ]


def _tile_loop(h_hbm, o_hbm, hbuf, obuf, in_sems, out_sems, tile_fn):
    n_batch, seq, _ = h_hbm.shape
    tm = hbuf.shape[1]
    tiles_per_seq = seq // tm
    n_tiles = n_batch * tiles_per_seq

    def hbm_tile(ref, t):
        return ref.at[t // tiles_per_seq, pl.ds(pl.multiple_of((t % tiles_per_seq) * tm, tm), tm), :]

    def in_copy(t, slot):
        return pltpu.make_async_copy(hbm_tile(h_hbm, t), hbuf.at[slot], in_sems.at[slot])

    def out_copy(t, slot):
        return pltpu.make_async_copy(obuf.at[slot], hbm_tile(o_hbm, t), out_sems.at[slot])

    in_copy(0, 0).start()

    def step(t, carry):
        slot = t % 2
        in_copy(t, slot).wait()

        @pl.when(t + 1 < n_tiles)
        def _():
            in_copy(t + 1, 1 - slot).start()

        @pl.when(t >= 2)
        def _():
            out_copy(t - 2, slot).wait()

        tile_fn(t % tiles_per_seq == 0, hbuf.at[slot], obuf.at[slot])
        out_copy(t, slot).start()
        return carry

    lax.fori_loop(0, n_tiles, step, 0)
    for t in range(max(n_tiles - 2, 0), n_tiles):
        out_copy(t, t % 2).wait()


def _tile_buffers(tm, d):
    return [
        pltpu.VMEM((2, tm, d), F32),
        pltpu.VMEM((2, tm, d), F32),
        pltpu.SemaphoreType.DMA((2,)),
        pltpu.SemaphoreType.DMA((2,)),
    ]


_VMEM_SPEC = pl.BlockSpec(memory_space=pltpu.VMEM)
_HBM_SPEC = pl.BlockSpec(memory_space=pl.ANY)


def _compiler_params():
    return pltpu.CompilerParams(vmem_limit_bytes=VMEM_LIMIT_BYTES)


def _sgu_kernel(layer, h_hbm, mg_ref, win_hbm, vg_ref, ws_ref, bs_ref, wout_hbm, o_hbm,
                u_scr, v_scr, us_scr, wsm_scr, win_ref, win_stage, win_sems, wout_ref, wout_stage, wout_sems,
                hbuf, obuf, in_sems, out_sems):
    _fetch_as_bf16(win_hbm.at[layer], win_ref, win_stage, win_sems)
    _fetch_as_bf16(wout_hbm.at[layer], wout_ref, wout_stage, wout_sems)
    row = lax.broadcasted_iota(jnp.int32, (CHUNK, CHUNK), 0)
    col = lax.broadcasted_iota(jnp.int32, (CHUNK, CHUNK), 1)
    for g in range(SGU_GROUPS):
        wsm_scr[g] = jnp.where(row >= col, ws_ref[g], 0.0).astype(BF16)

    def tile(first_tile, h_ref, o_ref):
        del first_tile
        _sgu_tile(h_ref, mg_ref, vg_ref, bs_ref, o_ref, u_scr, v_scr, us_scr, wsm_scr, win_ref, wout_ref)

    _tile_loop(h_hbm, o_hbm, hbuf, obuf, in_sems, out_sems, tile)


def _sgu_tile(h_ref, mg_ref, vg_ref, bs_ref, o_ref, u_scr, v_scr, us_scr, wsm_scr, win_ref, wout_ref):
    tm = h_ref.shape[0]
    width = wout_ref.shape[0]
    h = h_ref[...]
    hn = _rms_norm(h, mg_ref[...]).astype(BF16)

    v_sumsq = []

    def in_proj(c0):
        z = jnp.dot(hn, win_ref[:, c0:c0 + SGU_COL_TILE], preferred_element_type=F32)
        yield
        z = 0.5 * z * (1.0 + lax.erf(z * np.float32(math.sqrt(0.5))))
        if c0 < width:
            u_scr[:, c0:c0 + SGU_COL_TILE] = z
        else:
            v_sumsq.append(jnp.sum(z * z, axis=-1, keepdims=True))
            v_scr[:, c0 - width:c0 - width + SGU_COL_TILE] = z

    _interleave([(in_proj(c0), ()) for c0 in range(width, 2 * width, SGU_COL_TILE)], CHAINS_IN_FLIGHT)
    rstd = lax.rsqrt(sum(v_sumsq) * np.float32(1.0 / width) + EPS)

    n_chunk = tm // CHUNK

    def gate(g):
        g0 = g * LANES
        vn = jnp.concatenate(
            [(v_scr[cb * CHUNK:(cb + 1) * CHUNK, g0:g0 + LANES] * rstd[cb * CHUNK:(cb + 1) * CHUNK]
              * vg_ref[:, g0:g0 + LANES]).astype(BF16) for cb in range(n_chunk)], axis=1)
        yield
        s = jnp.dot(wsm_scr[g], vn, preferred_element_type=F32) + bs_ref[:, g:g + 1]
        yield
        for cb in range(n_chunk):
            r0 = cb * CHUNK
            us_scr[r0:r0 + CHUNK, g0:g0 + LANES] = (
                u_scr[r0:r0 + CHUNK, g0:g0 + LANES] * s[:, cb * CHUNK:(cb + 1) * CHUNK]).astype(BF16)

    u_projs = [in_proj(c0) for c0 in range(0, width, SGU_COL_TILE)]
    groups_per_tile = SGU_COL_TILE // LANES
    chains = [(u_projs[0], ())]
    for i, u_proj in enumerate(u_projs):
        if i + 1 < len(u_projs):
            chains.append((u_projs[i + 1], ()))
        chains.extend((gate(g), (u_proj,)) for g in range(i * groups_per_tile, (i + 1) * groups_per_tile))
    _interleave(chains, CHAINS_IN_FLIGHT)

    o_ref[...] = h + jnp.dot(us_scr[:, 0:width], wout_ref[...], preferred_element_type=F32)


def _sgu_layer(layer, h, mix_gain, w_in_all, v_gain, w_s, b_s, w_out_all):
    B, S, D = h.shape
    tm = min(TOKEN_TILE, S)
    width = w_out_all.shape[1]
    return pl.pallas_call(
        functools.partial(_sgu_kernel, layer),
        in_specs=[_HBM_SPEC, _VMEM_SPEC, _HBM_SPEC, _VMEM_SPEC, _VMEM_SPEC, _VMEM_SPEC, _HBM_SPEC],
        out_specs=_HBM_SPEC,
        out_shape=jax.ShapeDtypeStruct((B, S, D), F32),
        scratch_shapes=[
            pltpu.VMEM((tm, width + LANES), F32),
            pltpu.VMEM((tm, width + LANES), F32),
            pltpu.VMEM((tm, width + LANES), BF16),
            pltpu.VMEM((SGU_GROUPS, CHUNK, CHUNK), BF16),
        ] + _weight_scratch(D, 2 * width) + _weight_scratch(width, D) + _tile_buffers(tm, D),
        compiler_params=_compiler_params(),
        name="sgu_mixer",
    )(h, mix_gain.reshape(1, D), w_in_all, v_gain.reshape(1, width), w_s, b_s.T, w_out_all)


def _ffn_kernel(layer, h_hbm, ng_ref, wup_hbm, cw_ref, cb_ref, wdown_hbm, o_hbm,
                carry_scr, act_scr, perm_scr, wup_ref, wup_stage, wup_sems, wdown_ref, wdown_stage, wdown_sems,
                hbuf, obuf, in_sems, out_sems):
    _fetch_as_bf16(wup_hbm.at[layer], wup_ref, wup_stage, wup_sems)
    _fetch_as_bf16(wdown_hbm.at[layer], wdown_ref, wdown_stage, wdown_sems)

    def tile(first_tile, h_ref, o_ref):
        _ffn_tile(first_tile, h_ref, ng_ref, cw_ref, cb_ref, o_ref, carry_scr, act_scr, perm_scr, wup_ref, wdown_ref)

    _tile_loop(h_hbm, o_hbm, hbuf, obuf, in_sems, out_sems, tile)


def _ffn_tile(first_tile, h_ref, ng_ref, cw_ref, cb_ref, o_ref, carry_scr, act_scr, perm_scr, wup_ref, wdown_ref):
    n_sub, tm, d_ff = act_scr.shape
    d = h_ref.shape[1]
    seg = tm // SUBLANES
    pitch = perm_scr.shape[2] // SUBLANES
    ct = FFN_COL_TILE
    n_col = d_ff // ct
    n_slab = d // LANES

    @pl.when(first_tile)
    def _():
        carry_scr[...] = jnp.zeros_like(carry_scr)

    first_row = lax.broadcasted_iota(jnp.int32, (seg, 2 * ct), 0) == 0
    h_pm, hn_pm, out_pm = {}, {}, {}

    def to_phase_major(i):
        for r in range(seg):
            for c in range(n_slab):
                perm_scr[i, c, pl.ds(r, SUBLANES, stride=pitch), :] = (
                    h_ref[i * tm + r * SUBLANES:i * tm + (r + 1) * SUBLANES, c * LANES:(c + 1) * LANES])
        yield
        h_pm[i] = jnp.concatenate(
            [jnp.concatenate([perm_scr[i, c, p * pitch:p * pitch + seg, :] for c in range(n_slab)], axis=1)
             for p in range(SUBLANES)], axis=0)
        hn_pm[i] = _rms_norm(h_pm[i], ng_ref[...]).astype(BF16)

    def pair(ref, rows, c):
        return jnp.concatenate([ref[rows, c * ct:(c + 1) * ct], ref[rows, d_ff + c * ct:d_ff + (c + 1) * ct]],
                               axis=1)

    def up_tile(i, c):
        c0 = 2 * ct * c
        a = jnp.dot(hn_pm[i], pair(wup_ref, slice(None), c), preferred_element_type=F32)
        yield
        wrap1 = jnp.where(first_row, carry_scr[1:2, c0:c0 + 2 * ct], pltpu.roll(a[7 * seg:8 * seg], 1, 0))
        wrap2 = jnp.where(first_row, carry_scr[0:1, c0:c0 + 2 * ct], pltpu.roll(a[6 * seg:7 * seg], 1, 0))
        carry_scr[0:1, c0:c0 + 2 * ct] = a[7 * seg - 1:7 * seg]
        carry_scr[1:2, c0:c0 + 2 * ct] = a[8 * seg - 1:8 * seg]
        a1 = jnp.concatenate([wrap1, a[0:7 * seg]], axis=0)
        a2 = jnp.concatenate([wrap2, wrap1, a[0:6 * seg]], axis=0)
        conv = (pair(cw_ref, slice(0, 1), c) * a2 + pair(cw_ref, slice(1, 2), c) * a1
                + pair(cw_ref, slice(2, 3), c) * a + pair(cb_ref, slice(None), c))
        yield
        gate = conv[:, 0:ct]
        val = conv[:, ct:2 * ct]
        half_gate = 0.5 * gate
        act_scr[i, :, c * ct:(c + 1) * ct] = ((half_gate + half_gate * jnp.tanh(half_gate)) * val).astype(BF16)

    def down(i, n0):
        n1 = n0 + FFN_OUT_TILE
        out = h_pm[i][:, n0:n1] + jnp.dot(act_scr[i], wdown_ref[:, n0:n1], preferred_element_type=F32)
        yield
        for p in range(SUBLANES):
            for c in range(n0 // LANES, n1 // LANES):
                perm_scr[i, c, p * pitch:p * pitch + seg, :] = (
                    out[p * seg:(p + 1) * seg, c * LANES - n0:(c + 1) * LANES - n0])
        yield
        for r in range(seg):
            for c in range(n0 // LANES, n1 // LANES):
                o_ref[i * tm + r * SUBLANES:i * tm + (r + 1) * SUBLANES, c * LANES:(c + 1) * LANES] = (
                    perm_scr[i, c, pl.ds(r, SUBLANES, stride=pitch), :])

    prep = [to_phase_major(i) for i in range(n_sub)]
    ups = [[up_tile(i, c) for c in range(n_col)] for i in range(n_sub)]
    chains = []
    for i in range(n_sub):
        chains.append((prep[i], ()))
        chains.extend((ups[i][c], (prep[i],) + ((ups[i - 1][c],) if i > 0 else ())) for c in range(n_col))
        chains.extend((down(i, n0), tuple(ups[i])) for n0 in range(0, d, FFN_OUT_TILE))
    _interleave(chains, CHAINS_IN_FLIGHT)


def _ffn_layer(layer, h, norm_gain, w_up_all, conv_w, conv_b, w_down_all):
    B, S, D = h.shape
    tm = min(TOKEN_TILE, S)
    n_sub = min(FFN_TILES_PER_STEP, S // tm)
    d_ff = w_down_all.shape[1]
    return pl.pallas_call(
        functools.partial(_ffn_kernel, layer),
        in_specs=[_HBM_SPEC, _VMEM_SPEC, _HBM_SPEC, _VMEM_SPEC, _VMEM_SPEC, _HBM_SPEC],
        out_specs=_HBM_SPEC,
        out_shape=jax.ShapeDtypeStruct((B, S, D), F32),
        scratch_shapes=[
            pltpu.VMEM((SUBLANES, 2 * d_ff), F32),
            pltpu.VMEM((n_sub, tm, d_ff), BF16),
            pltpu.VMEM((n_sub, D // LANES, tm + SUBLANES * SUBLANES, LANES), F32),
        ] + _weight_scratch(D, 2 * d_ff) + _weight_scratch(d_ff, D) + _tile_buffers(n_sub * tm, D),
        compiler_params=_compiler_params(),
        name="conv_glu_ffn",
    )(h, norm_gain.reshape(1, D), w_up_all, conv_w, conv_b.reshape(1, 2 * d_ff), w_down_all)


def _band_rel_buckets():
    dist = (np.arange(CHUNK)[:, None] + CHUNK) - np.arange(2 * CHUNK)[None, :]
    n = np.maximum(dist, 0)
    max_exact = REL_BUCKETS // 2
    large = max_exact + (np.log(np.maximum(n, 1).astype(np.float32) / max_exact)
                         / math.log(REL_MAX_DIST / max_exact)
                         * (REL_BUCKETS - max_exact)).astype(np.int32)
    large = np.minimum(large, REL_BUCKETS - 1)
    bucket = np.where(n < max_exact, n, large)
    return np.where((dist >= 0) & (dist < CHUNK), bucket, -1).astype(np.int32)


def _bias_kernel(bucket_ref, rb_ref, o_ref):
    bucket = bucket_ref[...]
    for head in range(N_HEADS):
        acc = jnp.full(bucket.shape, -jnp.inf, F32)
        for b in range(REL_BUCKETS):
            acc = jnp.where(bucket == b, rb_ref[b, head], acc)
        kvh, pair, odd = head // KV_GROUP, (head % KV_GROUP) // 2, head % 2
        o_ref[kvh, odd * 2 * CHUNK:(odd + 1) * 2 * CHUNK, pair * CHUNK:(pair + 1) * CHUNK] = acc.T * LOG2_E


def _bias_table(rel_bias):
    bucket = jnp.asarray(_band_rel_buckets())
    return pl.pallas_call(
        _bias_kernel,
        in_specs=[
            pl.BlockSpec(memory_space=pltpu.VMEM),
            pl.BlockSpec(memory_space=pltpu.SMEM),
        ],
        out_specs=pl.BlockSpec(memory_space=pltpu.VMEM),
        out_shape=jax.ShapeDtypeStruct((N_KV_HEADS, 4 * CHUNK, 2 * CHUNK), F32),
        name="rel_bias_table",
    )(bucket, rel_bias)


def _head_sumsq(x, seg):
    return jnp.dot((x * x).astype(BF16), seg, preferred_element_type=F32)


def _swa_kernel(layer, h_hbm, mg_ref, wqkv_hbm, qg_ref, kg_ref, sink_ref, bias_ref, seg_ref, wo_hbm, o_hbm,
                klo_scr, khi_scr, vt_scr, q_scr, att_scr,
                wqkv_ref, wqkv_stage, wqkv_sems, wo_ref, wo_stage, wo_sems,
                hbuf, obuf, in_sems, out_sems):
    _fetch_as_bf16(wqkv_hbm.at[layer], wqkv_ref, wqkv_stage, wqkv_sems)
    _fetch_as_bf16(wo_hbm.at[layer], wo_ref, wo_stage, wo_sems)

    def tile(first_tile, h_ref, o_ref):
        _swa_tile(first_tile, h_ref, mg_ref, qg_ref, kg_ref, sink_ref, bias_ref, seg_ref, o_ref,
                  klo_scr, khi_scr, vt_scr, q_scr, att_scr, wqkv_ref, wo_ref)

    _tile_loop(h_hbm, o_hbm, hbuf, obuf, in_sems, out_sems, tile)


def _swa_tile(first_tile, h_ref, mg_ref, qg_ref, kg_ref, sink_ref, bias_ref, seg_ref, o_ref,
              klo_scr, khi_scr, vt_scr, q_scr, att_scr, wqkv_ref, wo_ref):
    tm = h_ref.shape[0]
    dq = N_HEADS * HEAD_DIM
    dkv = N_KV_HEADS * HEAD_DIM
    vt_rows = 3 * HEAD_DIM

    @pl.when(first_tile)
    def _():
        klo_scr[0:CHUNK, :] = jnp.zeros((CHUNK, klo_scr.shape[1]), BF16)
        khi_scr[0:CHUNK, :] = jnp.zeros((CHUNK, khi_scr.shape[1]), BF16)
        vt_scr[...] = jnp.zeros(vt_scr.shape, BF16)

    def store_lo_hi(lo_scr, hi_scr, x):
        zeros = jnp.zeros((tm, HEAD_DIM), F32)
        for kvh in range(N_KV_HEADS):
            x_h = x[:, kvh * HEAD_DIM:(kvh + 1) * HEAD_DIM]
            lo_scr[CHUNK:CHUNK + tm, kvh * LANES:(kvh + 1) * LANES] = (
                jnp.concatenate([x_h, zeros], axis=1).astype(BF16))
            hi_scr[CHUNK:CHUNK + tm, kvh * LANES:(kvh + 1) * LANES] = (
                jnp.concatenate([zeros, x_h], axis=1).astype(BF16))

    def store_transposed(x):
        xt = x.T
        for kvh in range(N_KV_HEADS):
            vt_scr[kvh * vt_rows + HEAD_DIM:kvh * vt_rows + 2 * HEAD_DIM, CHUNK:CHUNK + tm] = (
                xt[kvh * HEAD_DIM:(kvh + 1) * HEAD_DIM].astype(BF16))

    h = h_ref[...]
    hn = _rms_norm(h, mg_ref[...]).astype(BF16)
    seg = seg_ref[...]
    inv_hd = np.float32(1.0 / HEAD_DIM)

    def project(w0):
        xw = jnp.dot(hn, wqkv_ref[:, w0:w0 + 2 * QK_COL_TILE], preferred_element_type=F32)
        yield
        for c0 in range(w0, w0 + 2 * QK_COL_TILE, QK_COL_TILE):
            x = xw[:, c0 - w0:c0 - w0 + QK_COL_TILE]
            if c0 >= dq + dkv:
                store_transposed(x)
                yield
                continue
            r = lax.rsqrt(_head_sumsq(x, seg) * inv_hd + EPS)
            yield
            if c0 < dq:
                gain = qg_ref[:, c0:c0 + QK_COL_TILE] * (np.float32(HEAD_DIM ** -0.5) * LOG2_E)
                q_scr[:, c0:c0 + QK_COL_TILE] = (x * r * gain).astype(BF16)
            else:
                store_lo_hi(klo_scr, khi_scr, x * r * kg_ref[:, c0 - dq:c0 - dq + QK_COL_TILE])
            yield

    band_row = lax.broadcasted_iota(jnp.int32, (4 * CHUNK, 1), 0) % (2 * CHUNK)
    no_prev = jnp.where(first_tile & (band_row < CHUNK), -jnp.inf, 0.0).astype(F32)
    upper_pair = lax.broadcasted_iota(jnp.int32, (1, 2 * CHUNK), 1) >= CHUNK

    def scores(blk, kvh):
        r0 = blk * CHUNK
        c0 = kvh * KV_GROUP * HEAD_DIM
        q2 = jnp.concatenate([q_scr[r0:r0 + CHUNK, c0:c0 + LANES],
                              q_scr[r0:r0 + CHUNK, c0 + LANES:c0 + 2 * LANES]], axis=0)
        k2 = jnp.concatenate([klo_scr[r0:r0 + 2 * CHUNK, kvh * LANES:(kvh + 1) * LANES],
                              khi_scr[r0:r0 + 2 * CHUNK, kvh * LANES:(kvh + 1) * LANES]], axis=0)
        return lax.dot_general(k2, q2, (((1,), (1,)), ((), ())), preferred_element_type=F32)

    def attend(blk, kvh):
        r0 = blk * CHUNK
        c0 = kvh * KV_GROUP * HEAD_DIM
        s = scores(blk, kvh)
        yield
        s = s + bias_ref[kvh]
        if blk == 0:
            s = s + no_prev
        probs, inv_denoms = [], []
        for odd in range(2):
            t = s[odd * 2 * CHUNK:(odd + 1) * 2 * CHUNK]
            sink = LOG2_E * jnp.where(upper_pair, sink_ref[kvh * KV_GROUP + 2 + odd],
                                      sink_ref[kvh * KV_GROUP + odd])
            m = jnp.maximum(jnp.max(t, axis=0, keepdims=True), sink)
            yield
            p = jnp.exp2(t - m)
            inv_denoms.append(1.0 / (jnp.sum(p, axis=0, keepdims=True) + jnp.exp2(sink - m)))
            probs.append(p.astype(BF16))
            yield
        v0 = kvh * vt_rows
        vt2 = jnp.concatenate([vt_scr[v0 + HEAD_DIM:v0 + 3 * HEAD_DIM, r0:r0 + 2 * CHUNK],
                               vt_scr[v0:v0 + 2 * HEAD_DIM, r0:r0 + 2 * CHUNK]], axis=1)
        ot = jnp.dot(vt2, jnp.concatenate(probs, axis=0), preferred_element_type=F32)
        yield
        ot = jnp.concatenate([ot[0:HEAD_DIM] * inv_denoms[0], ot[HEAD_DIM:2 * HEAD_DIM] * inv_denoms[1]], axis=0)
        att_scr[r0:r0 + CHUNK, c0:c0 + LANES] = ot[:, 0:CHUNK].T.astype(BF16)
        att_scr[r0:r0 + CHUNK, c0 + LANES:c0 + 2 * LANES] = ot[:, CHUNK:2 * CHUNK].T.astype(BF16)

    kv_proj = project(dq)
    q_projs = [project(w0) for w0 in range(0, dq, 2 * QK_COL_TILE)]
    kvh_per_proj = 2 * QK_COL_TILE // (KV_GROUP * HEAD_DIM)
    chains = [(kv_proj, ()), (q_projs[0], ())]
    for kvh in range(N_KV_HEADS):
        chains.extend((attend(blk, kvh), (kv_proj, q_projs[kvh // kvh_per_proj])) for blk in range(tm // CHUNK))
        nxt = kvh // kvh_per_proj + 1
        if kvh % kvh_per_proj == 0 and nxt < len(q_projs):
            chains.append((q_projs[nxt], ()))
    _interleave(chains, CHAINS_IN_FLIGHT)

    kv_cols = N_KV_HEADS * LANES
    klo_scr[0:CHUNK, 0:kv_cols] = klo_scr[tm:tm + CHUNK, 0:kv_cols]
    khi_scr[0:CHUNK, 0:kv_cols] = khi_scr[tm:tm + CHUNK, 0:kv_cols]
    vt_scr[:, 0:CHUNK] = vt_scr[:, tm:tm + CHUNK]

    o_ref[...] = h + jnp.dot(att_scr[:, 0:dq], wo_ref[...], preferred_element_type=F32)


def _swa_layer(layer, h, mix_gain, w_qkv_all, q_gain, k_gain, sinks, w_o_all, bias_table):
    B, S, D = h.shape
    tm = min(TOKEN_TILE, S)
    dq = N_HEADS * HEAD_DIM
    dkv = N_KV_HEADS * HEAD_DIM
    lane_head = np.arange(QK_COL_TILE) // HEAD_DIM
    seg = jnp.asarray(lane_head[:, None] == lane_head[None, :], BF16)
    return pl.pallas_call(
        functools.partial(_swa_kernel, layer),
        in_specs=[_HBM_SPEC, _VMEM_SPEC, _HBM_SPEC, _VMEM_SPEC, _VMEM_SPEC,
                  pl.BlockSpec(memory_space=pltpu.SMEM), _VMEM_SPEC, _VMEM_SPEC, _HBM_SPEC],
        out_specs=_HBM_SPEC,
        out_shape=jax.ShapeDtypeStruct((B, S, D), F32),
        scratch_shapes=[
            pltpu.VMEM((CHUNK + tm, (N_KV_HEADS + 1) * LANES), BF16),
            pltpu.VMEM((CHUNK + tm, (N_KV_HEADS + 1) * LANES), BF16),
            pltpu.VMEM((N_KV_HEADS * 3 * HEAD_DIM, CHUNK + tm), BF16),
            pltpu.VMEM((tm, dq + LANES), BF16),
            pltpu.VMEM((tm, dq + LANES), BF16),
        ] + _weight_scratch(D, dq + 2 * dkv) + _weight_scratch(dq, D) + _tile_buffers(tm, D),
        compiler_params=_compiler_params(),
        name="swa_mixer",
    )(h, mix_gain.reshape(1, D), w_qkv_all, jnp.tile(q_gain, N_HEADS).reshape(1, dq),
      jnp.tile(k_gain, N_KV_HEADS).reshape(1, dkv), sinks, bias_table, seg, w_o_all)


def kernel(x, mix_norm, ffn_norm, sgu_w_in, sgu_v_gain, sgu_w_s, sgu_b_s, sgu_w_out, attn_w_qkv,
           attn_q_gain, attn_k_gain, attn_sinks, attn_w_o, rel_bias, ffn_w_up, ffn_conv_w, ffn_conv_b,
           ffn_w_down):
    depth = mix_norm.shape[0]
    bias_table = _bias_table(rel_bias) if depth > 1 else None
    h = x
    for i in range(depth):
        j = i // 2
        if i % 2 == 0:
            h = _sgu_layer(j, h, mix_norm[i], sgu_w_in, sgu_v_gain[j], sgu_w_s[j], sgu_b_s[j], sgu_w_out)
        else:
            h = _swa_layer(j, h, mix_norm[i], attn_w_qkv, attn_q_gain[j], attn_k_gain[j], attn_sinks[j],
                           attn_w_o, bias_table)
        h = _ffn_layer(i, h, ffn_norm[i], ffn_w_up, ffn_conv_w[i], ffn_conv_b[i], ffn_w_down)
    return h
```

```python
import functools
import math

import numpy as np
import jax
import jax.numpy as jnp
from jax import lax
from jax.experimental import pallas as pl
from jax.experimental.pallas import tpu as pltpu

F32 = jnp.float32
BF16 = jnp.bfloat16

EPS = 1e-6
LOG2_E = np.float32(1.0 / math.log(2.0))
LANES = 128
SUBLANES = 8
CHUNK = 128
HEAD_DIM = 64
N_HEADS = 16
N_KV_HEADS = 4
KV_GROUP = N_HEADS // N_KV_HEADS
SGU_GROUPS = 16
REL_BUCKETS = 32
REL_MAX_DIST = 128
CONV_WIDTH = 3

TOKEN_TILE = 512
SWA_TOKEN_TILE = 1024
FFN_TILES_PER_STEP = 1
FETCH_SLOTS = 3
FETCH_MAX_TILE_ROWS = 256
FETCH_MAX_TILE_BYTES = 3 * 512 * 1024
SGU_COL_TILE = 512
FFN_COL_TILE = 256
FFN_OUT_TILE = 1024
QK_COL_TILE = 256
CHAINS_IN_FLIGHT = 6
VMEM_LIMIT_BYTES = 56 * 1024 * 1024


def _rms_norm(x, gain):
    ms = jnp.mean(x * x, axis=-1, keepdims=True)
    return x * lax.rsqrt(ms + EPS) * gain


def _interleave(chains, in_flight):
    waiting = list(chains)
    running, finished = [], set()
    while waiting or running:
        if waiting and len(running) < in_flight and all(id(p) in finished for p in waiting[0][1]):
            running.append(waiting.pop(0)[0])
        for chain in list(running):
            if next(chain, "done") == "done":
                running.remove(chain)
                finished.add(id(chain))


def _fetch_as_bf16(src_hbm, dst_scr, stage, sems):
    n_slots, tile_rows, _ = stage.shape
    n_tiles = dst_scr.shape[0] // tile_rows

    def copy(k):
        return pltpu.make_async_copy(src_hbm.at[pl.ds(k * tile_rows, tile_rows), :],
                                     stage.at[k % n_slots], sems.at[k % n_slots])

    for k in range(min(n_slots - 1, n_tiles)):
        copy(k).start()
    for k in range(n_tiles):
        if k + n_slots - 1 < n_tiles:
            copy(k + n_slots - 1).start()
        copy(k).wait()
        dst_scr[k * tile_rows:(k + 1) * tile_rows, :] = stage[k % n_slots].astype(BF16)


def _weight_scratch(n_rows, n_cols):
    tile_rows = FETCH_MAX_TILE_ROWS
    while tile_rows * n_cols * 4 > FETCH_MAX_TILE_BYTES or n_rows % tile_rows:
        tile_rows //= 2
    assert tile_rows >= SUBLANES, (n_rows, n_cols)
    return [
        pltpu.VMEM((n_rows, n_cols), BF16),
        pltpu.VMEM((FETCH_SLOTS, tile_rows, n_cols), F32),
        pltpu.SemaphoreType.DMA((FETCH_SLOTS,)),
    ]


def _tile_loop(h_hbm, o_hbm, hbuf, obuf, in_sems, out_sems, tile_fn):
    n_batch, seq, _ = h_hbm.shape
    tm = hbuf.shape[1]
    tiles_per_seq = seq // tm
    n_tiles = n_batch * tiles_per_seq

    def hbm_tile(ref, t):
        return ref.at[t // tiles_per_seq, pl.ds(pl.multiple_of((t % tiles_per_seq) * tm, tm), tm), :]

    def in_copy(t, slot):
        return pltpu.make_async_copy(hbm_tile(h_hbm, t), hbuf.at[slot], in_sems.at[slot])

    def out_copy(t, slot):
        return pltpu.make_async_copy(obuf.at[slot], hbm_tile(o_hbm, t), out_sems.at[slot])

    in_copy(0, 0).start()

    def step(t, carry):
        slot = t % 2
        in_copy(t, slot).wait()

        @pl.when(t + 1 < n_tiles)
        def _():
            in_copy(t + 1, 1 - slot).start()

        @pl.when(t >= 2)
        def _():
            out_copy(t - 2, slot).wait()

        tile_fn(t % tiles_per_seq == 0, hbuf.at[slot], obuf.at[slot])
        out_copy(t, slot).start()
        return carry

    lax.fori_loop(0, n_tiles, step, 0)
    for t in range(max(n_tiles - 2, 0), n_tiles):
        out_copy(t, t % 2).wait()


def _tile_buffers(tm, d):
    return [
        pltpu.VMEM((2, tm, d), F32),
        pltpu.VMEM((2, tm, d), F32),
        pltpu.SemaphoreType.DMA((2,)),
        pltpu.SemaphoreType.DMA((2,)),
    ]


_VMEM_SPEC = pl.BlockSpec(memory_space=pltpu.VMEM)
_HBM_SPEC = pl.BlockSpec(memory_space=pl.ANY)


def _compiler_params():
    return pltpu.CompilerParams(vmem_limit_bytes=VMEM_LIMIT_BYTES)


def _sgu_kernel(layer, h_hbm, mg_ref, win_hbm, vg_ref, ws_ref, bs_ref, wout_hbm, o_hbm,
                u_scr, v_scr, us_scr, wsm_scr, win_ref, win_stage, win_sems, wout_ref, wout_stage, wout_sems,
                hbuf, obuf, in_sems, out_sems):
    _fetch_as_bf16(win_hbm.at[layer], win_ref, win_stage, win_sems)
    _fetch_as_bf16(wout_hbm.at[layer], wout_ref, wout_stage, wout_sems)
    row = lax.broadcasted_iota(jnp.int32, (CHUNK, CHUNK), 0)
    col = lax.broadcasted_iota(jnp.int32, (CHUNK, CHUNK), 1)
    for g in range(SGU_GROUPS):
        wsm_scr[g] = jnp.where(row >= col, ws_ref[g], 0.0).astype(BF16)

    def tile(first_tile, h_ref, o_ref):
        del first_tile
        _sgu_tile(h_ref, mg_ref, vg_ref, bs_ref, o_ref, u_scr, v_scr, us_scr, wsm_scr, win_ref, wout_ref)

    _tile_loop(h_hbm, o_hbm, hbuf, obuf, in_sems, out_sems, tile)


def _sgu_tile(h_ref, mg_ref, vg_ref, bs_ref, o_ref, u_scr, v_scr, us_scr, wsm_scr, win_ref, wout_ref):
    tm = h_ref.shape[0]
    width = wout_ref.shape[0]
    h = h_ref[...]
    hn = _rms_norm(h, mg_ref[...]).astype(BF16)

    v_sumsq = []

    def in_proj(c0):
        z = jnp.dot(hn, win_ref[:, c0:c0 + SGU_COL_TILE], preferred_element_type=F32)
        yield
        z = 0.5 * z * (1.0 + lax.erf(z * np.float32(math.sqrt(0.5))))
        if c0 < width:
            u_scr[:, c0:c0 + SGU_COL_TILE] = z
        else:
            v_sumsq.append(jnp.sum(z * z, axis=-1, keepdims=True))
            v_scr[:, c0 - width:c0 - width + SGU_COL_TILE] = z

    _interleave([(in_proj(c0), ()) for c0 in range(width, 2 * width, SGU_COL_TILE)], CHAINS_IN_FLIGHT)
    rstd = lax.rsqrt(sum(v_sumsq) * np.float32(1.0 / width) + EPS)

    n_chunk = tm // CHUNK

    def gate(g):
        g0 = g * LANES
        vn = jnp.concatenate(
            [(v_scr[cb * CHUNK:(cb + 1) * CHUNK, g0:g0 + LANES] * rstd[cb * CHUNK:(cb + 1) * CHUNK]
              * vg_ref[:, g0:g0 + LANES]).astype(BF16) for cb in range(n_chunk)], axis=1)
        yield
        s = jnp.dot(wsm_scr[g], vn, preferred_element_type=F32) + bs_ref[:, g:g + 1]
        yield
        for cb in range(n_chunk):
            r0 = cb * CHUNK
            us_scr[r0:r0 + CHUNK, g0:g0 + LANES] = (
                u_scr[r0:r0 + CHUNK, g0:g0 + LANES] * s[:, cb * CHUNK:(cb + 1) * CHUNK]).astype(BF16)

    u_projs = [in_proj(c0) for c0 in range(0, width, SGU_COL_TILE)]
    groups_per_tile = SGU_COL_TILE // LANES
    chains = [(u_projs[0], ())]
    for i, u_proj in enumerate(u_projs):
        if i + 1 < len(u_projs):
            chains.append((u_projs[i + 1], ()))
        chains.extend((gate(g), (u_proj,)) for g in range(i * groups_per_tile, (i + 1) * groups_per_tile))
    _interleave(chains, CHAINS_IN_FLIGHT)

    o_ref[...] = h + jnp.dot(us_scr[:, 0:width], wout_ref[...], preferred_element_type=F32)


def _sgu_layer(layer, h, mix_gain, w_in_all, v_gain, w_s, b_s, w_out_all):
    B, S, D = h.shape
    tm = min(TOKEN_TILE, S)
    width = w_out_all.shape[1]
    return pl.pallas_call(
        functools.partial(_sgu_kernel, layer),
        in_specs=[_HBM_SPEC, _VMEM_SPEC, _HBM_SPEC, _VMEM_SPEC, _VMEM_SPEC, _VMEM_SPEC, _HBM_SPEC],
        out_specs=_HBM_SPEC,
        out_shape=jax.ShapeDtypeStruct((B, S, D), F32),
        scratch_shapes=[
            pltpu.VMEM((tm, width + LANES), F32),
            pltpu.VMEM((tm, width + LANES), F32),
            pltpu.VMEM((tm, width + LANES), BF16),
            pltpu.VMEM((SGU_GROUPS, CHUNK, CHUNK), BF16),
        ] + _weight_scratch(D, 2 * width) + _weight_scratch(width, D) + _tile_buffers(tm, D),
        compiler_params=_compiler_params(),
        name="sgu_mixer",
    )(h, mix_gain.reshape(1, D), w_in_all, v_gain.reshape(1, width), w_s, b_s.T, w_out_all)


def _ffn_kernel(layer, h_hbm, ng_ref, wup_hbm, cw_ref, cb_ref, wdown_hbm, o_hbm,
                carry_scr, act_scr, perm_scr, wup_ref, wup_stage, wup_sems, wdown_ref, wdown_stage, wdown_sems,
                hbuf, obuf, in_sems, out_sems):
    _fetch_as_bf16(wup_hbm.at[layer], wup_ref, wup_stage, wup_sems)
    _fetch_as_bf16(wdown_hbm.at[layer], wdown_ref, wdown_stage, wdown_sems)

    def tile(first_tile, h_ref, o_ref):
        _ffn_tile(first_tile, h_ref, ng_ref, cw_ref, cb_ref, o_ref, carry_scr, act_scr, perm_scr, wup_ref, wdown_ref)

    _tile_loop(h_hbm, o_hbm, hbuf, obuf, in_sems, out_sems, tile)


def _ffn_tile(first_tile, h_ref, ng_ref, cw_ref, cb_ref, o_ref, carry_scr, act_scr, perm_scr, wup_ref, wdown_ref):
    n_sub, tm, d_ff = act_scr.shape
    d = h_ref.shape[1]
    seg = tm // SUBLANES
    pitch = perm_scr.shape[2] // SUBLANES
    ct = FFN_COL_TILE
    n_col = d_ff // ct
    n_slab = d // LANES

    @pl.when(first_tile)
    def _():
        carry_scr[...] = jnp.zeros_like(carry_scr)

    first_row = lax.broadcasted_iota(jnp.int32, (seg, 2 * ct), 0) == 0
    h_pm, hn_pm, out_pm = {}, {}, {}

    def to_phase_major(i):
        for r in range(seg):
            for c in range(n_slab):
                perm_scr[i, c, pl.ds(r, SUBLANES, stride=pitch), :] = (
                    h_ref[i * tm + r * SUBLANES:i * tm + (r + 1) * SUBLANES, c * LANES:(c + 1) * LANES])
        yield
        h_pm[i] = jnp.concatenate(
            [jnp.concatenate([perm_scr[i, c, p * pitch:p * pitch + seg, :] for c in range(n_slab)], axis=1)
             for p in range(SUBLANES)], axis=0)
        hn_pm[i] = _rms_norm(h_pm[i], ng_ref[...]).astype(BF16)

    def pair(ref, rows, c):
        return jnp.concatenate([ref[rows, c * ct:(c + 1) * ct], ref[rows, d_ff + c * ct:d_ff + (c + 1) * ct]],
                               axis=1)

    def up_tile(i, c):
        c0 = 2 * ct * c
        a = jnp.dot(hn_pm[i], pair(wup_ref, slice(None), c), preferred_element_type=F32)
        yield
        wrap1 = jnp.where(first_row, carry_scr[1:2, c0:c0 + 2 * ct], pltpu.roll(a[7 * seg:8 * seg], 1, 0))
        wrap2 = jnp.where(first_row, carry_scr[0:1, c0:c0 + 2 * ct], pltpu.roll(a[6 * seg:7 * seg], 1, 0))
        carry_scr[0:1, c0:c0 + 2 * ct] = a[7 * seg - 1:7 * seg]
        carry_scr[1:2, c0:c0 + 2 * ct] = a[8 * seg - 1:8 * seg]
        a1 = jnp.concatenate([wrap1, a[0:7 * seg]], axis=0)
        a2 = jnp.concatenate([wrap2, wrap1, a[0:6 * seg]], axis=0)
        conv = (pair(cw_ref, slice(0, 1), c) * a2 + pair(cw_ref, slice(1, 2), c) * a1
                + pair(cw_ref, slice(2, 3), c) * a + pair(cb_ref, slice(None), c))
        yield
        gate = conv[:, 0:ct]
        val = conv[:, ct:2 * ct]
        act_scr[i, :, c * ct:(c + 1) * ct] = (gate / (1.0 + jnp.exp(-gate)) * val).astype(BF16)

    def down(i, n0):
        n1 = n0 + FFN_OUT_TILE
        out = h_pm[i][:, n0:n1] + jnp.dot(act_scr[i], wdown_ref[:, n0:n1], preferred_element_type=F32)
        yield
        for p in range(SUBLANES):
            for c in range(n0 // LANES, n1 // LANES):
                perm_scr[i, c, p * pitch:p * pitch + seg, :] = (
                    out[p * seg:(p + 1) * seg, c * LANES - n0:(c + 1) * LANES - n0])
        yield
        for r in range(seg):
            for c in range(n0 // LANES, n1 // LANES):
                o_ref[i * tm + r * SUBLANES:i * tm + (r + 1) * SUBLANES, c * LANES:(c + 1) * LANES] = (
                    perm_scr[i, c, pl.ds(r, SUBLANES, stride=pitch), :])

    prep = [to_phase_major(i) for i in range(n_sub)]
    ups = [[up_tile(i, c) for c in range(n_col)] for i in range(n_sub)]
    chains = []
    for i in range(n_sub):
        chains.append((prep[i], ()))
        chains.extend((ups[i][c], (prep[i],) + ((ups[i - 1][c],) if i > 0 else ())) for c in range(n_col))
        chains.extend((down(i, n0), tuple(ups[i])) for n0 in range(0, d, FFN_OUT_TILE))
    _interleave(chains, CHAINS_IN_FLIGHT)


def _ffn_layer(layer, h, norm_gain, w_up_all, conv_w, conv_b, w_down_all):
    B, S, D = h.shape
    tm = min(TOKEN_TILE, S)
    n_sub = min(FFN_TILES_PER_STEP, S // tm)
    d_ff = w_down_all.shape[1]
    return pl.pallas_call(
        functools.partial(_ffn_kernel, layer),
        in_specs=[_HBM_SPEC, _VMEM_SPEC, _HBM_SPEC, _VMEM_SPEC, _VMEM_SPEC, _HBM_SPEC],
        out_specs=_HBM_SPEC,
        out_shape=jax.ShapeDtypeStruct((B, S, D), F32),
        scratch_shapes=[
            pltpu.VMEM((SUBLANES, 2 * d_ff), F32),
            pltpu.VMEM((n_sub, tm, d_ff), BF16),
            pltpu.VMEM((n_sub, D // LANES, tm + SUBLANES * SUBLANES, LANES), F32),
        ] + _weight_scratch(D, 2 * d_ff) + _weight_scratch(d_ff, D) + _tile_buffers(n_sub * tm, D),
        compiler_params=_compiler_params(),
        name="conv_glu_ffn",
    )(h, norm_gain.reshape(1, D), w_up_all, conv_w, conv_b.reshape(1, 2 * d_ff), w_down_all)


def _band_rel_buckets():
    dist = (np.arange(CHUNK)[:, None] + CHUNK) - np.arange(2 * CHUNK)[None, :]
    n = np.maximum(dist, 0)
    max_exact = REL_BUCKETS // 2
    large = max_exact + (np.log(np.maximum(n, 1).astype(np.float32) / max_exact)
                         / math.log(REL_MAX_DIST / max_exact)
                         * (REL_BUCKETS - max_exact)).astype(np.int32)
    large = np.minimum(large, REL_BUCKETS - 1)
    bucket = np.where(n < max_exact, n, large)
    return np.where((dist >= 0) & (dist < CHUNK), bucket, -1).astype(np.int32)


def _bias_kernel(bucket_ref, rb_ref, o_ref):
    bucket = bucket_ref[...]
    for head in range(N_HEADS):
        acc = jnp.full(bucket.shape, -jnp.inf, F32)
        for b in range(REL_BUCKETS):
            acc = jnp.where(bucket == b, rb_ref[b, head], acc)
        kvh, pair, odd = head // KV_GROUP, (head % KV_GROUP) // 2, head % 2
        o_ref[kvh, odd * 2 * CHUNK:(odd + 1) * 2 * CHUNK, pair * CHUNK:(pair + 1) * CHUNK] = acc.T * LOG2_E


def _bias_table(rel_bias):
    bucket = jnp.asarray(_band_rel_buckets())
    return pl.pallas_call(
        _bias_kernel,
        in_specs=[
            pl.BlockSpec(memory_space=pltpu.VMEM),
            pl.BlockSpec(memory_space=pltpu.SMEM),
        ],
        out_specs=pl.BlockSpec(memory_space=pltpu.VMEM),
        out_shape=jax.ShapeDtypeStruct((N_KV_HEADS, 4 * CHUNK, 2 * CHUNK), F32),
        name="rel_bias_table",
    )(bucket, rel_bias)


def _head_sumsq(x, seg):
    return jnp.dot((x * x).astype(BF16), seg, preferred_element_type=F32)


def _swa_kernel(layer, h_hbm, mg_ref, wqkv_hbm, qg_ref, kg_ref, sink_ref, bias_ref, seg_ref, wo_hbm, o_hbm,
                klo_scr, khi_scr, vt_scr, q_scr, att_scr,
                wqkv_ref, wqkv_stage, wqkv_sems, wo_ref, wo_stage, wo_sems,
                hbuf, obuf, in_sems, out_sems):
    _fetch_as_bf16(wqkv_hbm.at[layer], wqkv_ref, wqkv_stage, wqkv_sems)
    _fetch_as_bf16(wo_hbm.at[layer], wo_ref, wo_stage, wo_sems)

    def tile(first_tile, h_ref, o_ref):
        _swa_tile(first_tile, h_ref, mg_ref, qg_ref, kg_ref, sink_ref, bias_ref, seg_ref, o_ref,
                  klo_scr, khi_scr, vt_scr, q_scr, att_scr, wqkv_ref, wo_ref)

    _tile_loop(h_hbm, o_hbm, hbuf, obuf, in_sems, out_sems, tile)


def _swa_tile(first_tile, h_ref, mg_ref, qg_ref, kg_ref, sink_ref, bias_ref, seg_ref, o_ref,
              klo_scr, khi_scr, vt_scr, q_scr, att_scr, wqkv_ref, wo_ref):
    tm = h_ref.shape[0]
    dq = N_HEADS * HEAD_DIM
    dkv = N_KV_HEADS * HEAD_DIM
    vt_rows = 3 * HEAD_DIM

    @pl.when(first_tile)
    def _():
        klo_scr[0:CHUNK, :] = jnp.zeros((CHUNK, klo_scr.shape[1]), BF16)
        khi_scr[0:CHUNK, :] = jnp.zeros((CHUNK, khi_scr.shape[1]), BF16)
        vt_scr[...] = jnp.zeros(vt_scr.shape, BF16)

    def store_lo_hi(lo_scr, hi_scr, x):
        zeros = jnp.zeros((tm, HEAD_DIM), F32)
        for kvh in range(N_KV_HEADS):
            x_h = x[:, kvh * HEAD_DIM:(kvh + 1) * HEAD_DIM]
            lo_scr[CHUNK:CHUNK + tm, kvh * LANES:(kvh + 1) * LANES] = (
                jnp.concatenate([x_h, zeros], axis=1).astype(BF16))
            hi_scr[CHUNK:CHUNK + tm, kvh * LANES:(kvh + 1) * LANES] = (
                jnp.concatenate([zeros, x_h], axis=1).astype(BF16))

    def store_transposed(x):
        xt = x.T
        for kvh in range(N_KV_HEADS):
            vt_scr[kvh * vt_rows + HEAD_DIM:kvh * vt_rows + 2 * HEAD_DIM, CHUNK:CHUNK + tm] = (
                xt[kvh * HEAD_DIM:(kvh + 1) * HEAD_DIM].astype(BF16))

    h = h_ref[...]
    hn = _rms_norm(h, mg_ref[...]).astype(BF16)
    seg = seg_ref[...]
    inv_hd = np.float32(1.0 / HEAD_DIM)

    def project(w0):
        xw = jnp.dot(hn, wqkv_ref[:, w0:w0 + 2 * QK_COL_TILE], preferred_element_type=F32)
        yield
        for c0 in range(w0, w0 + 2 * QK_COL_TILE, QK_COL_TILE):
            x = xw[:, c0 - w0:c0 - w0 + QK_COL_TILE]
            if c0 >= dq + dkv:
                store_transposed(x)
                yield
                continue
            r = lax.rsqrt(_head_sumsq(x, seg) * inv_hd + EPS)
            yield
            if c0 < dq:
                gain = qg_ref[:, c0:c0 + QK_COL_TILE] * (np.float32(HEAD_DIM ** -0.5) * LOG2_E)
                q_scr[:, c0:c0 + QK_COL_TILE] = (x * r * gain).astype(BF16)
            else:
                store_lo_hi(klo_scr, khi_scr, x * r * kg_ref[:, c0 - dq:c0 - dq + QK_COL_TILE])
            yield

    band_row = lax.broadcasted_iota(jnp.int32, (4 * CHUNK, 1), 0) % (2 * CHUNK)
    no_prev = jnp.where(first_tile & (band_row < CHUNK), -jnp.inf, 0.0).astype(F32)
    upper_pair = lax.broadcasted_iota(jnp.int32, (1, 2 * CHUNK), 1) >= CHUNK

    def scores(blk, kvh):
        r0 = blk * CHUNK
        c0 = kvh * KV_GROUP * HEAD_DIM
        q2 = jnp.concatenate([q_scr[r0:r0 + CHUNK, c0:c0 + LANES],
                              q_scr[r0:r0 + CHUNK, c0 + LANES:c0 + 2 * LANES]], axis=0)
        k2 = jnp.concatenate([klo_scr[r0:r0 + 2 * CHUNK, kvh * LANES:(kvh + 1) * LANES],
                              khi_scr[r0:r0 + 2 * CHUNK, kvh * LANES:(kvh + 1) * LANES]], axis=0)
        return lax.dot_general(k2, q2, (((1,), (1,)), ((), ())), preferred_element_type=F32)

    def attend(blk, kvh):
        r0 = blk * CHUNK
        c0 = kvh * KV_GROUP * HEAD_DIM
        s = scores(blk, kvh)
        yield
        s = s + bias_ref[kvh]
        if blk == 0:
            s = s + no_prev
        probs, inv_denoms = [], []
        for odd in range(2):
            t = s[odd * 2 * CHUNK:(odd + 1) * 2 * CHUNK]
            sink = LOG2_E * jnp.where(upper_pair, sink_ref[kvh * KV_GROUP + 2 + odd],
                                      sink_ref[kvh * KV_GROUP + odd])
            m = jnp.maximum(jnp.max(t, axis=0, keepdims=True), sink)
            yield
            p = jnp.exp2(t - m)
            inv_denoms.append(1.0 / (jnp.sum(p, axis=0, keepdims=True) + jnp.exp2(sink - m)))
            probs.append(p.astype(BF16))
            yield
        v0 = kvh * vt_rows
        vt2 = jnp.concatenate([vt_scr[v0 + HEAD_DIM:v0 + 3 * HEAD_DIM, r0:r0 + 2 * CHUNK],
                               vt_scr[v0:v0 + 2 * HEAD_DIM, r0:r0 + 2 * CHUNK]], axis=1)
        ot = jnp.dot(vt2, jnp.concatenate(probs, axis=0), preferred_element_type=F32)
        yield
        ot = jnp.concatenate([ot[0:HEAD_DIM] * inv_denoms[0], ot[HEAD_DIM:2 * HEAD_DIM] * inv_denoms[1]], axis=0)
        att_scr[r0:r0 + CHUNK, c0:c0 + LANES] = ot[:, 0:CHUNK].T.astype(BF16)
        att_scr[r0:r0 + CHUNK, c0 + LANES:c0 + 2 * LANES] = ot[:, CHUNK:2 * CHUNK].T.astype(BF16)

    kv_proj = project(dq)
    q_projs = [project(w0) for w0 in range(0, dq, 2 * QK_COL_TILE)]
    kvh_per_proj = 2 * QK_COL_TILE // (KV_GROUP * HEAD_DIM)
    chains = [(kv_proj, ()), (q_projs[0], ())]
    for kvh in range(N_KV_HEADS):
        chains.extend((attend(blk, kvh), (kv_proj, q_projs[kvh // kvh_per_proj])) for blk in range(tm // CHUNK))
        nxt = kvh // kvh_per_proj + 1
        if kvh % kvh_per_proj == 0 and nxt < len(q_projs):
            chains.append((q_projs[nxt], ()))
    _interleave(chains, CHAINS_IN_FLIGHT)

    kv_cols = N_KV_HEADS * LANES
    klo_scr[0:CHUNK, 0:kv_cols] = klo_scr[tm:tm + CHUNK, 0:kv_cols]
    khi_scr[0:CHUNK, 0:kv_cols] = khi_scr[tm:tm + CHUNK, 0:kv_cols]
    vt_scr[:, 0:CHUNK] = vt_scr[:, tm:tm + CHUNK]

    o_ref[...] = h + jnp.dot(att_scr[:, 0:dq], wo_ref[...], preferred_element_type=F32)


def _swa_layer(layer, h, mix_gain, w_qkv_all, q_gain, k_gain, sinks, w_o_all, bias_table):
    B, S, D = h.shape
    tm = min(SWA_TOKEN_TILE, S)
    dq = N_HEADS * HEAD_DIM
    dkv = N_KV_HEADS * HEAD_DIM
    lane_head = np.arange(QK_COL_TILE) // HEAD_DIM
    seg = jnp.asarray(lane_head[:, None] == lane_head[None, :], BF16)
    return pl.pallas_call(
        functools.partial(_swa_kernel, layer),
        in_specs=[_HBM_SPEC, _VMEM_SPEC, _HBM_SPEC, _VMEM_SPEC, _VMEM_SPEC,
                  pl.BlockSpec(memory_space=pltpu.SMEM), _VMEM_SPEC, _VMEM_SPEC, _HBM_SPEC],
        out_specs=_HBM_SPEC,
        out_shape=jax.ShapeDtypeStruct((B, S, D), F32),
        scratch_shapes=[
            pltpu.VMEM((CHUNK + tm, (N_KV_HEADS + 1) * LANES), BF16),
            pltpu.VMEM((CHUNK + tm, (N_KV_HEADS + 1) * LANES), BF16),
            pltpu.VMEM((N_KV_HEADS * 3 * HEAD_DIM, CHUNK + tm), BF16),
            pltpu.VMEM((tm, dq + LANES), BF16),
            pltpu.VMEM((tm, dq + LANES), BF16),
        ] + _weight_scratch(D, dq + 2 * dkv) + _weight_scratch(dq, D) + _tile_buffers(tm, D),
        compiler_params=_compiler_params(),
        name="swa_mixer",
    )(h, mix_gain.reshape(1, D), w_qkv_all, jnp.tile(q_gain, N_HEADS).reshape(1, dq),
      jnp.tile(k_gain, N_KV_HEADS).reshape(1, dkv), sinks, bias_table, seg, w_o_all)


def kernel(x, mix_norm, ffn_norm, sgu_w_in, sgu_v_gain, sgu_w_s, sgu_b_s, sgu_w_out, attn_w_qkv,
           attn_q_gain, attn_k_gain, attn_sinks, attn_w_o, rel_bias, ffn_w_up, ffn_conv_w, ffn_conv_b,
           ffn_w_down):
    depth = mix_norm.shape[0]
    bias_table = _bias_table(rel_bias) if depth > 1 else None
    h = x
    for i in range(depth):
        j = i // 2
        if i % 2 == 0:
            h = _sgu_layer(j, h, mix_norm[i], sgu_w_in, sgu_v_gain[j], sgu_w_s[j], sgu_b_s[j], sgu_w_out)
        else:
            h = _swa_layer(j, h, mix_norm[i], attn_w_qkv, attn_q_gain[j], attn_k_gain[j], attn_sinks[j],
                           attn_w_o, bias_table)
        h = _ffn_layer(i, h, ffn_norm[i], ffn_w_up, ffn_conv_w[i], ffn_conv_b[i], ffn_w_down)
    return h
```

```python
import functools
import math

import numpy as np
import jax
import jax.numpy as jnp
from jax import lax
from jax.experimental import pallas as pl
from jax.experimental.pallas import tpu as pltpu

F32 = jnp.float32
BF16 = jnp.bfloat16

EPS = 1e-6
LOG2_E = np.float32(1.0 / math.log(2.0))
LANES = 128
SUBLANES = 8
CHUNK = 128
HEAD_DIM = 64
N_HEADS = 16
N_KV_HEADS = 4
KV_GROUP = N_HEADS // N_KV_HEADS
SGU_GROUPS = 16
REL_BUCKETS = 32
REL_MAX_DIST = 128
CONV_WIDTH = 3

TOKEN_TILE = 512
SWA_TOKEN_TILE = 1024
FFN_TOKEN_TILE = 1024
FFN_TILES_PER_STEP = 1
FETCH_SLOTS = 3
FETCH_MAX_TILE_ROWS = 256
FETCH_MAX_TILE_BYTES = 3 * 512 * 1024
SGU_COL_TILE = 512
FFN_COL_TILE = 256
FFN_OUT_TILE = 1024
QK_COL_TILE = 256
CHAINS_IN_FLIGHT = 6
VMEM_LIMIT_BYTES = 56 * 1024 * 1024


def _rms_norm(x, gain):
    ms = jnp.mean(x * x, axis=-1, keepdims=True)
    return x * lax.rsqrt(ms + EPS) * gain


def _interleave(chains, in_flight):
    waiting = list(chains)
    running, finished = [], set()
    while waiting or running:
        if waiting and len(running) < in_flight and all(id(p) in finished for p in waiting[0][1]):
            running.append(waiting.pop(0)[0])
        for chain in list(running):
            if next(chain, "done") == "done":
                running.remove(chain)
                finished.add(id(chain))


def _fetch_as_bf16(src_hbm, dst_scr, stage, sems):
    n_slots, tile_rows, _ = stage.shape
    n_tiles = dst_scr.shape[0] // tile_rows

    def copy(k):
        return pltpu.make_async_copy(src_hbm.at[pl.ds(k * tile_rows, tile_rows), :],
                                     stage.at[k % n_slots], sems.at[k % n_slots])

    for k in range(min(n_slots - 1, n_tiles)):
        copy(k).start()
    for k in range(n_tiles):
        if k + n_slots - 1 < n_tiles:
            copy(k + n_slots - 1).start()
        copy(k).wait()
        dst_scr[k * tile_rows:(k + 1) * tile_rows, :] = stage[k % n_slots].astype(BF16)


def _weight_scratch(n_rows, n_cols):
    tile_rows = FETCH_MAX_TILE_ROWS
    while tile_rows * n_cols * 4 > FETCH_MAX_TILE_BYTES or n_rows % tile_rows:
        tile_rows //= 2
    assert tile_rows >= SUBLANES, (n_rows, n_cols)
    return [
        pltpu.VMEM((n_rows, n_cols), BF16),
        pltpu.VMEM((FETCH_SLOTS, tile_rows, n_cols), F32),
        pltpu.SemaphoreType.DMA((FETCH_SLOTS,)),
    ]


def _tile_loop(h_hbm, o_hbm, hbuf, obuf, in_sems, out_sems, tile_fn):
    n_batch, seq, _ = h_hbm.shape
    tm = hbuf.shape[1]
    tiles_per_seq = seq // tm
    n_tiles = n_batch * tiles_per_seq

    def hbm_tile(ref, t):
        return ref.at[t // tiles_per_seq, pl.ds(pl.multiple_of((t % tiles_per_seq) * tm, tm), tm), :]

    def in_copy(t, slot):
        return pltpu.make_async_copy(hbm_tile(h_hbm, t), hbuf.at[slot], in_sems.at[slot])

    def out_copy(t, slot):
        return pltpu.make_async_copy(obuf.at[slot], hbm_tile(o_hbm, t), out_sems.at[slot])

    in_copy(0, 0).start()

    def step(t, carry):
        slot = t % 2
        in_copy(t, slot).wait()

        @pl.when(t + 1 < n_tiles)
        def _():
            in_copy(t + 1, 1 - slot).start()

        @pl.when(t >= 2)
        def _():
            out_copy(t - 2, slot).wait()

        tile_fn(t % tiles_per_seq == 0, hbuf.at[slot], obuf.at[slot])
        out_copy(t, slot).start()
        return carry

    lax.fori_loop(0, n_tiles, step, 0)
    for t in range(max(n_tiles - 2, 0), n_tiles):
        out_copy(t, t % 2).wait()


def _tile_buffers(tm, d):
    return [
        pltpu.VMEM((2, tm, d), F32),
        pltpu.VMEM((2, tm, d), F32),
        pltpu.SemaphoreType.DMA((2,)),
        pltpu.SemaphoreType.DMA((2,)),
    ]


_VMEM_SPEC = pl.BlockSpec(memory_space=pltpu.VMEM)
_HBM_SPEC = pl.BlockSpec(memory_space=pl.ANY)


def _compiler_params():
    return pltpu.CompilerParams(vmem_limit_bytes=VMEM_LIMIT_BYTES)


def _sgu_kernel(layer, h_hbm, mg_ref, win_hbm, vg_ref, ws_ref, bs_ref, wout_hbm, o_hbm,
                u_scr, v_scr, us_scr, wsm_scr, win_ref, win_stage, win_sems, wout_ref, wout_stage, wout_sems,
                hbuf, obuf, in_sems, out_sems):
    _fetch_as_bf16(win_hbm.at[layer], win_ref, win_stage, win_sems)
    _fetch_as_bf16(wout_hbm.at[layer], wout_ref, wout_stage, wout_sems)
    row = lax.broadcasted_iota(jnp.int32, (CHUNK, CHUNK), 0)
    col = lax.broadcasted_iota(jnp.int32, (CHUNK, CHUNK), 1)
    for g in range(SGU_GROUPS):
        wsm_scr[g] = jnp.where(row >= col, ws_ref[g], 0.0).astype(BF16)

    def tile(first_tile, h_ref, o_ref):
        del first_tile
        _sgu_tile(h_ref, mg_ref, vg_ref, bs_ref, o_ref, u_scr, v_scr, us_scr, wsm_scr, win_ref, wout_ref)

    _tile_loop(h_hbm, o_hbm, hbuf, obuf, in_sems, out_sems, tile)


def _sgu_tile(h_ref, mg_ref, vg_ref, bs_ref, o_ref, u_scr, v_scr, us_scr, wsm_scr, win_ref, wout_ref):
    tm = h_ref.shape[0]
    width = wout_ref.shape[0]
    h = h_ref[...]
    hn = _rms_norm(h, mg_ref[...]).astype(BF16)

    v_sumsq = []

    def in_proj(c0):
        z = jnp.dot(hn, win_ref[:, c0:c0 + SGU_COL_TILE], preferred_element_type=F32)
        yield
        z = 0.5 * z * (1.0 + lax.erf(z * np.float32(math.sqrt(0.5))))
        if c0 < width:
            u_scr[:, c0:c0 + SGU_COL_TILE] = z
        else:
            v_sumsq.append(jnp.sum(z * z, axis=-1, keepdims=True))
            v_scr[:, c0 - width:c0 - width + SGU_COL_TILE] = z

    _interleave([(in_proj(c0), ()) for c0 in range(width, 2 * width, SGU_COL_TILE)], CHAINS_IN_FLIGHT)
    rstd = lax.rsqrt(sum(v_sumsq) * np.float32(1.0 / width) + EPS)

    n_chunk = tm // CHUNK

    def gate(g):
        g0 = g * LANES
        vn = jnp.concatenate(
            [(v_scr[cb * CHUNK:(cb + 1) * CHUNK, g0:g0 + LANES] * rstd[cb * CHUNK:(cb + 1) * CHUNK]
              * vg_ref[:, g0:g0 + LANES]).astype(BF16) for cb in range(n_chunk)], axis=1)
        yield
        s = jnp.dot(wsm_scr[g], vn, preferred_element_type=F32) + bs_ref[:, g:g + 1]
        yield
        for cb in range(n_chunk):
            r0 = cb * CHUNK
            us_scr[r0:r0 + CHUNK, g0:g0 + LANES] = (
                u_scr[r0:r0 + CHUNK, g0:g0 + LANES] * s[:, cb * CHUNK:(cb + 1) * CHUNK]).astype(BF16)

    u_projs = [in_proj(c0) for c0 in range(0, width, SGU_COL_TILE)]
    groups_per_tile = SGU_COL_TILE // LANES
    chains = [(u_projs[0], ())]
    for i, u_proj in enumerate(u_projs):
        if i + 1 < len(u_projs):
            chains.append((u_projs[i + 1], ()))
        chains.extend((gate(g), (u_proj,)) for g in range(i * groups_per_tile, (i + 1) * groups_per_tile))
    _interleave(chains, CHAINS_IN_FLIGHT)

    o_ref[...] = h + jnp.dot(us_scr[:, 0:width], wout_ref[...], preferred_element_type=F32)


def _sgu_layer(layer, h, mix_gain, w_in_all, v_gain, w_s, b_s, w_out_all):
    B, S, D = h.shape
    tm = min(TOKEN_TILE, S)
    width = w_out_all.shape[1]
    return pl.pallas_call(
        functools.partial(_sgu_kernel, layer),
        in_specs=[_HBM_SPEC, _VMEM_SPEC, _HBM_SPEC, _VMEM_SPEC, _VMEM_SPEC, _VMEM_SPEC, _HBM_SPEC],
        out_specs=_HBM_SPEC,
        out_shape=jax.ShapeDtypeStruct((B, S, D), F32),
        scratch_shapes=[
            pltpu.VMEM((tm, width + LANES), F32),
            pltpu.VMEM((tm, width + LANES), F32),
            pltpu.VMEM((tm, width + LANES), BF16),
            pltpu.VMEM((SGU_GROUPS, CHUNK, CHUNK), BF16),
        ] + _weight_scratch(D, 2 * width) + _weight_scratch(width, D) + _tile_buffers(tm, D),
        compiler_params=_compiler_params(),
        name="sgu_mixer",
    )(h, mix_gain.reshape(1, D), w_in_all, v_gain.reshape(1, width), w_s, b_s.T, w_out_all)


def _ffn_kernel(layer, h_hbm, ng_ref, wup_hbm, cw_ref, cb_ref, wdown_hbm, o_hbm,
                carry_scr, act_scr, perm_scr, wup_ref, wup_stage, wup_sems, wdown_ref, wdown_stage, wdown_sems,
                hbuf, obuf, in_sems, out_sems):
    _fetch_as_bf16(wup_hbm.at[layer], wup_ref, wup_stage, wup_sems)
    _fetch_as_bf16(wdown_hbm.at[layer], wdown_ref, wdown_stage, wdown_sems)

    def tile(first_tile, h_ref, o_ref):
        _ffn_tile(first_tile, h_ref, ng_ref, cw_ref, cb_ref, o_ref, carry_scr, act_scr, perm_scr, wup_ref, wdown_ref)

    _tile_loop(h_hbm, o_hbm, hbuf, obuf, in_sems, out_sems, tile)


def _ffn_tile(first_tile, h_ref, ng_ref, cw_ref, cb_ref, o_ref, carry_scr, act_scr, perm_scr, wup_ref, wdown_ref):
    n_sub, tm, d_ff = act_scr.shape
    d = h_ref.shape[1]
    seg = tm // SUBLANES
    pitch = perm_scr.shape[2] // SUBLANES
    ct = FFN_COL_TILE
    n_col = d_ff // ct
    n_slab = d // LANES

    @pl.when(first_tile)
    def _():
        carry_scr[...] = jnp.zeros_like(carry_scr)

    first_row = lax.broadcasted_iota(jnp.int32, (seg, 2 * ct), 0) == 0
    h_pm, hn_pm, out_pm = {}, {}, {}

    def to_phase_major(i):
        for r in range(seg):
            for c in range(n_slab):
                perm_scr[i, c, pl.ds(r, SUBLANES, stride=pitch), :] = (
                    h_ref[i * tm + r * SUBLANES:i * tm + (r + 1) * SUBLANES, c * LANES:(c + 1) * LANES])
        yield
        h_pm[i] = jnp.concatenate(
            [jnp.concatenate([perm_scr[i, c, p * pitch:p * pitch + seg, :] for c in range(n_slab)], axis=1)
             for p in range(SUBLANES)], axis=0)
        hn_pm[i] = _rms_norm(h_pm[i], ng_ref[...]).astype(BF16)

    def pair(ref, rows, c):
        return jnp.concatenate([ref[rows, c * ct:(c + 1) * ct], ref[rows, d_ff + c * ct:d_ff + (c + 1) * ct]],
                               axis=1)

    def up_tile(i, c):
        c0 = 2 * ct * c
        a = jnp.dot(hn_pm[i], pair(wup_ref, slice(None), c), preferred_element_type=F32)
        yield
        wrap1 = jnp.where(first_row, carry_scr[1:2, c0:c0 + 2 * ct], pltpu.roll(a[7 * seg:8 * seg], 1, 0))
        wrap2 = jnp.where(first_row, carry_scr[0:1, c0:c0 + 2 * ct], pltpu.roll(a[6 * seg:7 * seg], 1, 0))
        carry_scr[0:1, c0:c0 + 2 * ct] = a[7 * seg - 1:7 * seg]
        carry_scr[1:2, c0:c0 + 2 * ct] = a[8 * seg - 1:8 * seg]
        a1 = jnp.concatenate([wrap1, a[0:7 * seg]], axis=0)
        a2 = jnp.concatenate([wrap2, wrap1, a[0:6 * seg]], axis=0)
        conv = (pair(cw_ref, slice(0, 1), c) * a2 + pair(cw_ref, slice(1, 2), c) * a1
                + pair(cw_ref, slice(2, 3), c) * a + pair(cb_ref, slice(None), c))
        yield
        gate = conv[:, 0:ct]
        val = conv[:, ct:2 * ct]
        act_scr[i, :, c * ct:(c + 1) * ct] = (gate / (1.0 + jnp.exp(-gate)) * val).astype(BF16)

    def down(i, n0):
        n1 = n0 + FFN_OUT_TILE
        out = h_pm[i][:, n0:n1] + jnp.dot(act_scr[i], wdown_ref[:, n0:n1], preferred_element_type=F32)
        yield
        for p in range(SUBLANES):
            for c in range(n0 // LANES, n1 // LANES):
                perm_scr[i, c, p * pitch:p * pitch + seg, :] = (
                    out[p * seg:(p + 1) * seg, c * LANES - n0:(c + 1) * LANES - n0])
        yield
        for r in range(seg):
            for c in range(n0 // LANES, n1 // LANES):
                o_ref[i * tm + r * SUBLANES:i * tm + (r + 1) * SUBLANES, c * LANES:(c + 1) * LANES] = (
                    perm_scr[i, c, pl.ds(r, SUBLANES, stride=pitch), :])

    prep = [to_phase_major(i) for i in range(n_sub)]
    ups = [[up_tile(i, c) for c in range(n_col)] for i in range(n_sub)]
    chains = []
    for i in range(n_sub):
        chains.append((prep[i], ()))
        chains.extend((ups[i][c], (prep[i],) + ((ups[i - 1][c],) if i > 0 else ())) for c in range(n_col))
        chains.extend((down(i, n0), tuple(ups[i])) for n0 in range(0, d, FFN_OUT_TILE))
    _interleave(chains, CHAINS_IN_FLIGHT)


def _ffn_layer(layer, h, norm_gain, w_up_all, conv_w, conv_b, w_down_all):
    B, S, D = h.shape
    tm = min(FFN_TOKEN_TILE, S)
    n_sub = min(FFN_TILES_PER_STEP, S // tm)
    d_ff = w_down_all.shape[1]
    return pl.pallas_call(
        functools.partial(_ffn_kernel, layer),
        in_specs=[_HBM_SPEC, _VMEM_SPEC, _HBM_SPEC, _VMEM_SPEC, _VMEM_SPEC, _HBM_SPEC],
        out_specs=_HBM_SPEC,
        out_shape=jax.ShapeDtypeStruct((B, S, D), F32),
        scratch_shapes=[
            pltpu.VMEM((SUBLANES, 2 * d_ff), F32),
            pltpu.VMEM((n_sub, tm, d_ff), BF16),
            pltpu.VMEM((n_sub, D // LANES, tm + SUBLANES * SUBLANES, LANES), F32),
        ] + _weight_scratch(D, 2 * d_ff) + _weight_scratch(d_ff, D) + _tile_buffers(n_sub * tm, D),
        compiler_params=_compiler_params(),
        name="conv_glu_ffn",
    )(h, norm_gain.reshape(1, D), w_up_all, conv_w, conv_b.reshape(1, 2 * d_ff), w_down_all)


def _band_rel_buckets():
    dist = (np.arange(CHUNK)[:, None] + CHUNK) - np.arange(2 * CHUNK)[None, :]
    n = np.maximum(dist, 0)
    max_exact = REL_BUCKETS // 2
    large = max_exact + (np.log(np.maximum(n, 1).astype(np.float32) / max_exact)
                         / math.log(REL_MAX_DIST / max_exact)
                         * (REL_BUCKETS - max_exact)).astype(np.int32)
    large = np.minimum(large, REL_BUCKETS - 1)
    bucket = np.where(n < max_exact, n, large)
    return np.where((dist >= 0) & (dist < CHUNK), bucket, -1).astype(np.int32)


def _bias_kernel(bucket_ref, rb_ref, o_ref):
    bucket = bucket_ref[...]
    for head in range(N_HEADS):
        acc = jnp.full(bucket.shape, -jnp.inf, F32)
        for b in range(REL_BUCKETS):
            acc = jnp.where(bucket == b, rb_ref[b, head], acc)
        kvh, pair, odd = head // KV_GROUP, (head % KV_GROUP) // 2, head % 2
        o_ref[kvh, odd * 2 * CHUNK:(odd + 1) * 2 * CHUNK, pair * CHUNK:(pair + 1) * CHUNK] = acc.T * LOG2_E


def _bias_table(rel_bias):
    bucket = jnp.asarray(_band_rel_buckets())
    return pl.pallas_call(
        _bias_kernel,
        in_specs=[
            pl.BlockSpec(memory_space=pltpu.VMEM),
            pl.BlockSpec(memory_space=pltpu.SMEM),
        ],
        out_specs=pl.BlockSpec(memory_space=pltpu.VMEM),
        out_shape=jax.ShapeDtypeStruct((N_KV_HEADS, 4 * CHUNK, 2 * CHUNK), F32),
        name="rel_bias_table",
    )(bucket, rel_bias)


def _head_sumsq(x, seg):
    return jnp.dot((x * x).astype(BF16), seg, preferred_element_type=F32)


def _swa_kernel(layer, h_hbm, mg_ref, wqkv_hbm, qg_ref, kg_ref, sink_ref, bias_ref, seg_ref, wo_hbm, o_hbm,
                klo_scr, khi_scr, vt_scr, q_scr, att_scr,
                wqkv_ref, wqkv_stage, wqkv_sems, wo_ref, wo_stage, wo_sems,
                hbuf, obuf, in_sems, out_sems):
    _fetch_as_bf16(wqkv_hbm.at[layer], wqkv_ref, wqkv_stage, wqkv_sems)
    _fetch_as_bf16(wo_hbm.at[layer], wo_ref, wo_stage, wo_sems)

    def tile(first_tile, h_ref, o_ref):
        _swa_tile(first_tile, h_ref, mg_ref, qg_ref, kg_ref, sink_ref, bias_ref, seg_ref, o_ref,
                  klo_scr, khi_scr, vt_scr, q_scr, att_scr, wqkv_ref, wo_ref)

    _tile_loop(h_hbm, o_hbm, hbuf, obuf, in_sems, out_sems, tile)


def _swa_tile(first_tile, h_ref, mg_ref, qg_ref, kg_ref, sink_ref, bias_ref, seg_ref, o_ref,
              klo_scr, khi_scr, vt_scr, q_scr, att_scr, wqkv_ref, wo_ref):
    tm = h_ref.shape[0]
    dq = N_HEADS * HEAD_DIM
    dkv = N_KV_HEADS * HEAD_DIM
    vt_rows = 3 * HEAD_DIM

    @pl.when(first_tile)
    def _():
        klo_scr[0:CHUNK, :] = jnp.zeros((CHUNK, klo_scr.shape[1]), BF16)
        khi_scr[0:CHUNK, :] = jnp.zeros((CHUNK, khi_scr.shape[1]), BF16)
        vt_scr[...] = jnp.zeros(vt_scr.shape, BF16)

    def store_lo_hi(lo_scr, hi_scr, x):
        zeros = jnp.zeros((tm, HEAD_DIM), F32)
        for kvh in range(N_KV_HEADS):
            x_h = x[:, kvh * HEAD_DIM:(kvh + 1) * HEAD_DIM]
            lo_scr[CHUNK:CHUNK + tm, kvh * LANES:(kvh + 1) * LANES] = (
                jnp.concatenate([x_h, zeros], axis=1).astype(BF16))
            hi_scr[CHUNK:CHUNK + tm, kvh * LANES:(kvh + 1) * LANES] = (
                jnp.concatenate([zeros, x_h], axis=1).astype(BF16))

    def store_transposed(x):
        xt = x.T
        for kvh in range(N_KV_HEADS):
            vt_scr[kvh * vt_rows + HEAD_DIM:kvh * vt_rows + 2 * HEAD_DIM, CHUNK:CHUNK + tm] = (
                xt[kvh * HEAD_DIM:(kvh + 1) * HEAD_DIM].astype(BF16))

    h = h_ref[...]
    hn = _rms_norm(h, mg_ref[...]).astype(BF16)
    seg = seg_ref[...]
    inv_hd = np.float32(1.0 / HEAD_DIM)

    def project(w0):
        xw = jnp.dot(hn, wqkv_ref[:, w0:w0 + 2 * QK_COL_TILE], preferred_element_type=F32)
        yield
        for c0 in range(w0, w0 + 2 * QK_COL_TILE, QK_COL_TILE):
            x = xw[:, c0 - w0:c0 - w0 + QK_COL_TILE]
            if c0 >= dq + dkv:
                store_transposed(x)
                yield
                continue
            r = lax.rsqrt(_head_sumsq(x, seg) * inv_hd + EPS)
            yield
            if c0 < dq:
                gain = qg_ref[:, c0:c0 + QK_COL_TILE] * (np.float32(HEAD_DIM ** -0.5) * LOG2_E)
                q_scr[:, c0:c0 + QK_COL_TILE] = (x * r * gain).astype(BF16)
            else:
                store_lo_hi(klo_scr, khi_scr, x * r * kg_ref[:, c0 - dq:c0 - dq + QK_COL_TILE])
            yield

    band_row = lax.broadcasted_iota(jnp.int32, (4 * CHUNK, 1), 0) % (2 * CHUNK)
    no_prev = jnp.where(first_tile & (band_row < CHUNK), -jnp.inf, 0.0).astype(F32)
    upper_pair = lax.broadcasted_iota(jnp.int32, (1, 2 * CHUNK), 1) >= CHUNK

    def scores(blk, kvh):
        r0 = blk * CHUNK
        c0 = kvh * KV_GROUP * HEAD_DIM
        q2 = jnp.concatenate([q_scr[r0:r0 + CHUNK, c0:c0 + LANES],
                              q_scr[r0:r0 + CHUNK, c0 + LANES:c0 + 2 * LANES]], axis=0)
        k2 = jnp.concatenate([klo_scr[r0:r0 + 2 * CHUNK, kvh * LANES:(kvh + 1) * LANES],
                              khi_scr[r0:r0 + 2 * CHUNK, kvh * LANES:(kvh + 1) * LANES]], axis=0)
        return lax.dot_general(k2, q2, (((1,), (1,)), ((), ())), preferred_element_type=F32)

    def attend(blk, kvh):
        r0 = blk * CHUNK
        c0 = kvh * KV_GROUP * HEAD_DIM
        s = scores(blk, kvh)
        yield
        s = s + bias_ref[kvh]
        if blk == 0:
            s = s + no_prev
        probs, inv_denoms = [], []
        for odd in range(2):
            t = s[odd * 2 * CHUNK:(odd + 1) * 2 * CHUNK]
            sink = LOG2_E * jnp.where(upper_pair, sink_ref[kvh * KV_GROUP + 2 + odd],
                                      sink_ref[kvh * KV_GROUP + odd])
            m = jnp.maximum(jnp.max(t, axis=0, keepdims=True), sink)
            yield
            p = jnp.exp2(t - m)
            inv_denoms.append(1.0 / (jnp.sum(p, axis=0, keepdims=True) + jnp.exp2(sink - m)))
            probs.append(p.astype(BF16))
            yield
        v0 = kvh * vt_rows
        vt2 = jnp.concatenate([vt_scr[v0 + HEAD_DIM:v0 + 3 * HEAD_DIM, r0:r0 + 2 * CHUNK],
                               vt_scr[v0:v0 + 2 * HEAD_DIM, r0:r0 + 2 * CHUNK]], axis=1)
        ot = jnp.dot(vt2, jnp.concatenate(probs, axis=0), preferred_element_type=F32)
        yield
        ot = jnp.concatenate([ot[0:HEAD_DIM] * inv_denoms[0], ot[HEAD_DIM:2 * HEAD_DIM] * inv_denoms[1]], axis=0)
        att_scr[r0:r0 + CHUNK, c0:c0 + LANES] = ot[:, 0:CHUNK].T.astype(BF16)
        att_scr[r0:r0 + CHUNK, c0 + LANES:c0 + 2 * LANES] = ot[:, CHUNK:2 * CHUNK].T.astype(BF16)

    kv_proj = project(dq)
    q_projs = [project(w0) for w0 in range(0, dq, 2 * QK_COL_TILE)]
    kvh_per_proj = 2 * QK_COL_TILE // (KV_GROUP * HEAD_DIM)
    chains = [(kv_proj, ()), (q_projs[0], ())]
    for kvh in range(N_KV_HEADS):
        chains.extend((attend(blk, kvh), (kv_proj, q_projs[kvh // kvh_per_proj])) for blk in range(tm // CHUNK))
        nxt = kvh // kvh_per_proj + 1
        if kvh % kvh_per_proj == 0 and nxt < len(q_projs):
            chains.append((q_projs[nxt], ()))
    _interleave(chains, CHAINS_IN_FLIGHT)

    kv_cols = N_KV_HEADS * LANES
    klo_scr[0:CHUNK, 0:kv_cols] = klo_scr[tm:tm + CHUNK, 0:kv_cols]
    khi_scr[0:CHUNK, 0:kv_cols] = khi_scr[tm:tm + CHUNK, 0:kv_cols]
    vt_scr[:, 0:CHUNK] = vt_scr[:, tm:tm + CHUNK]

    o_ref[...] = h + jnp.dot(att_scr[:, 0:dq], wo_ref[...], preferred_element_type=F32)


def _swa_layer(layer, h, mix_gain, w_qkv_all, q_gain, k_gain, sinks, w_o_all, bias_table):
    B, S, D = h.shape
    tm = min(SWA_TOKEN_TILE, S)
    dq = N_HEADS * HEAD_DIM
    dkv = N_KV_HEADS * HEAD_DIM
    lane_head = np.arange(QK_COL_TILE) // HEAD_DIM
    seg = jnp.asarray(lane_head[:, None] == lane_head[None, :], BF16)
    return pl.pallas_call(
        functools.partial(_swa_kernel, layer),
        in_specs=[_HBM_SPEC, _VMEM_SPEC, _HBM_SPEC, _VMEM_SPEC, _VMEM_SPEC,
                  pl.BlockSpec(memory_space=pltpu.SMEM), _VMEM_SPEC, _VMEM_SPEC, _HBM_SPEC],
        out_specs=_HBM_SPEC,
        out_shape=jax.ShapeDtypeStruct((B, S, D), F32),
        scratch_shapes=[
            pltpu.VMEM((CHUNK + tm, (N_KV_HEADS + 1) * LANES), BF16),
            pltpu.VMEM((CHUNK + tm, (N_KV_HEADS + 1) * LANES), BF16),
            pltpu.VMEM((N_KV_HEADS * 3 * HEAD_DIM, CHUNK + tm), BF16),
            pltpu.VMEM((tm, dq + LANES), BF16),
            pltpu.VMEM((tm, dq + LANES), BF16),
        ] + _weight_scratch(D, dq + 2 * dkv) + _weight_scratch(dq, D) + _tile_buffers(tm, D),
        compiler_params=_compiler_params(),
        name="swa_mixer",
    )(h, mix_gain.reshape(1, D), w_qkv_all, jnp.tile(q_gain, N_HEADS).reshape(1, dq),
      jnp.tile(k_gain, N_KV_HEADS).reshape(1, dkv), sinks, bias_table, seg, w_o_all)


def kernel(x, mix_norm, ffn_norm, sgu_w_in, sgu_v_gain, sgu_w_s, sgu_b_s, sgu_w_out, attn_w_qkv,
           attn_q_gain, attn_k_gain, attn_sinks, attn_w_o, rel_bias, ffn_w_up, ffn_conv_w, ffn_conv_b,
           ffn_w_down):
    depth = mix_norm.shape[0]
    bias_table = _bias_table(rel_bias) if depth > 1 else None
    h = x
    for i in range(depth):
        j = i // 2
        if i % 2 == 0:
            h = _sgu_layer(j, h, mix_norm[i], sgu_w_in, sgu_v_gain[j], sgu_w_s[j], sgu_b_s[j], sgu_w_out)
        else:
            h = _swa_layer(j, h, mix_norm[i], attn_w_qkv, attn_q_gain[j], attn_k_gain[j], attn_sinks[j],
                           attn_w_o, bias_table)
        h = _ffn_layer(i, h, ffn_norm[i], ffn_w_up, ffn_conv_w[i], ffn_conv_b[i], ffn_w_down)
    return h
```

```python
import functools
import math

import numpy as np
import jax
import jax.numpy as jnp
from jax import lax
from jax.experimental import pallas as pl
from jax.experimental.pallas import tpu as pltpu

F32 = jnp.float32
BF16 = jnp.bfloat16

EPS = 1e-6
LOG2_E = np.float32(1.0 / math.log(2.0))
LANES = 128
SUBLANES = 8
CHUNK = 128
HEAD_DIM = 64
N_HEADS = 16
N_KV_HEADS = 4
KV_GROUP = N_HEADS // N_KV_HEADS
SGU_GROUPS = 16
REL_BUCKETS = 32
REL_MAX_DIST = 128
CONV_WIDTH = 3

TOKEN_TILE = 1024
FFN_TILES_PER_STEP = 1
FETCH_SLOTS = 3
FETCH_MAX_TILE_ROWS = 256
FETCH_MAX_TILE_BYTES = 3 * 512 * 1024
SGU_COL_TILE = 512
FFN_COL_TILE = 256
FFN_OUT_TILE = 1024
QK_COL_TILE = 256
CHAINS_IN_FLIGHT = 6
VMEM_LIMIT_BYTES = 56 * 1024 * 1024


def _rms_norm(x, gain):
    ms = jnp.mean(x * x, axis=-1, keepdims=True)
    return x * lax.rsqrt(ms + EPS) * gain


def _interleave(chains, in_flight):
    waiting = list(chains)
    running, finished = [], set()
    while waiting or running:
        if waiting and len(running) < in_flight and all(id(p) in finished for p in waiting[0][1]):
            running.append(waiting.pop(0)[0])
        for chain in list(running):
            if next(chain, "done") == "done":
                running.remove(chain)
                finished.add(id(chain))


def _fetch_as_bf16(src_hbm, dst_scr, stage, sems):
    n_slots, tile_rows, _ = stage.shape
    n_tiles = dst_scr.shape[0] // tile_rows

    def copy(k):
        return pltpu.make_async_copy(src_hbm.at[pl.ds(k * tile_rows, tile_rows), :],
                                     stage.at[k % n_slots], sems.at[k % n_slots])

    for k in range(min(n_slots - 1, n_tiles)):
        copy(k).start()
    for k in range(n_tiles):
        if k + n_slots - 1 < n_tiles:
            copy(k + n_slots - 1).start()
        copy(k).wait()
        dst_scr[k * tile_rows:(k + 1) * tile_rows, :] = stage[k % n_slots].astype(BF16)


def _weight_scratch(n_rows, n_cols):
    tile_rows = FETCH_MAX_TILE_ROWS
    while tile_rows * n_cols * 4 > FETCH_MAX_TILE_BYTES or n_rows % tile_rows:
        tile_rows //= 2
    assert tile_rows >= SUBLANES, (n_rows, n_cols)
    return [
        pltpu.VMEM((n_rows, n_cols), BF16),
        pltpu.VMEM((FETCH_SLOTS, tile_rows, n_cols), F32),
        pltpu.SemaphoreType.DMA((FETCH_SLOTS,)),
    ]


def _tile_loop(h_hbm, o_hbm, hbuf, obuf, in_sems, out_sems, tile_fn):
    n_batch, seq, _ = h_hbm.shape
    tm = hbuf.shape[1]
    tiles_per_seq = seq // tm
    n_tiles = n_batch * tiles_per_seq

    def hbm_tile(ref, t):
        return ref.at[t // tiles_per_seq, pl.ds(pl.multiple_of((t % tiles_per_seq) * tm, tm), tm), :]

    def in_copy(t, slot):
        return pltpu.make_async_copy(hbm_tile(h_hbm, t), hbuf.at[slot], in_sems.at[slot])

    def out_copy(t, slot):
        return pltpu.make_async_copy(obuf.at[slot], hbm_tile(o_hbm, t), out_sems.at[slot])

    in_copy(0, 0).start()

    def step(t, carry):
        slot = t % 2
        in_copy(t, slot).wait()

        @pl.when(t + 1 < n_tiles)
        def _():
            in_copy(t + 1, 1 - slot).start()

        @pl.when(t >= 2)
        def _():
            out_copy(t - 2, slot).wait()

        tile_fn(t % tiles_per_seq == 0, hbuf.at[slot], obuf.at[slot])
        out_copy(t, slot).start()
        return carry

    lax.fori_loop(0, n_tiles, step, 0)
    for t in range(max(n_tiles - 2, 0), n_tiles):
        out_copy(t, t % 2).wait()


def _tile_buffers(tm, d):
    return [
        pltpu.VMEM((2, tm, d), F32),
        pltpu.VMEM((2, tm, d), F32),
        pltpu.SemaphoreType.DMA((2,)),
        pltpu.SemaphoreType.DMA((2,)),
    ]


_VMEM_SPEC = pl.BlockSpec(memory_space=pltpu.VMEM)
_HBM_SPEC = pl.BlockSpec(memory_space=pl.ANY)


def _compiler_params():
    return pltpu.CompilerParams(vmem_limit_bytes=VMEM_LIMIT_BYTES)


def _sgu_kernel(layer, h_hbm, mg_ref, win_hbm, vg_ref, ws_ref, bs_ref, wout_hbm, o_hbm,
                u_scr, v_scr, us_scr, wsm_scr, win_ref, win_stage, win_sems, wout_ref, wout_stage, wout_sems,
                hbuf, obuf, in_sems, out_sems):
    _fetch_as_bf16(win_hbm.at[layer], win_ref, win_stage, win_sems)
    _fetch_as_bf16(wout_hbm.at[layer], wout_ref, wout_stage, wout_sems)
    row = lax.broadcasted_iota(jnp.int32, (CHUNK, CHUNK), 0)
    col = lax.broadcasted_iota(jnp.int32, (CHUNK, CHUNK), 1)
    for g in range(SGU_GROUPS):
        wsm_scr[g] = jnp.where(row >= col, ws_ref[g], 0.0).astype(BF16)

    def tile(first_tile, h_ref, o_ref):
        del first_tile
        _sgu_tile(h_ref, mg_ref, vg_ref, bs_ref, o_ref, u_scr, v_scr, us_scr, wsm_scr, win_ref, wout_ref)

    _tile_loop(h_hbm, o_hbm, hbuf, obuf, in_sems, out_sems, tile)


def _sgu_tile(h_ref, mg_ref, vg_ref, bs_ref, o_ref, u_scr, v_scr, us_scr, wsm_scr, win_ref, wout_ref):
    tm = h_ref.shape[0]
    width = wout_ref.shape[0]
    h = h_ref[...]
    hn = _rms_norm(h, mg_ref[...]).astype(BF16)

    v_sumsq = []

    def in_proj(c0):
        z = jnp.dot(hn, win_ref[:, c0:c0 + SGU_COL_TILE], preferred_element_type=F32)
        yield
        z = 0.5 * z * (1.0 + lax.erf(z * np.float32(math.sqrt(0.5))))
        if c0 < width:
            u_scr[:, c0:c0 + SGU_COL_TILE] = z.astype(BF16)
        else:
            v_sumsq.append(jnp.sum(z * z, axis=-1, keepdims=True))
            v_scr[:, c0 - width:c0 - width + SGU_COL_TILE] = z.astype(BF16)

    _interleave([(in_proj(c0), ()) for c0 in range(width, 2 * width, SGU_COL_TILE)], CHAINS_IN_FLIGHT)
    rstd = lax.rsqrt(sum(v_sumsq) * np.float32(1.0 / width) + EPS)

    n_chunk = tm // CHUNK

    def gate(g):
        g0 = g * LANES
        vn = jnp.concatenate(
            [(v_scr[cb * CHUNK:(cb + 1) * CHUNK, g0:g0 + LANES].astype(F32) * rstd[cb * CHUNK:(cb + 1) * CHUNK]
              * vg_ref[:, g0:g0 + LANES]).astype(BF16) for cb in range(n_chunk)], axis=1)
        yield
        s = jnp.dot(wsm_scr[g], vn, preferred_element_type=F32) + bs_ref[:, g:g + 1]
        yield
        for cb in range(n_chunk):
            r0 = cb * CHUNK
            us_scr[r0:r0 + CHUNK, g0:g0 + LANES] = (
                u_scr[r0:r0 + CHUNK, g0:g0 + LANES].astype(F32) * s[:, cb * CHUNK:(cb + 1) * CHUNK]).astype(BF16)

    u_projs = [in_proj(c0) for c0 in range(0, width, SGU_COL_TILE)]
    groups_per_tile = SGU_COL_TILE // LANES
    chains = [(u_projs[0], ())]
    for i, u_proj in enumerate(u_projs):
        if i + 1 < len(u_projs):
            chains.append((u_projs[i + 1], ()))
        chains.extend((gate(g), (u_proj,)) for g in range(i * groups_per_tile, (i + 1) * groups_per_tile))
    _interleave(chains, CHAINS_IN_FLIGHT)

    o_ref[...] = h + jnp.dot(us_scr[:, 0:width], wout_ref[...], preferred_element_type=F32)


def _sgu_layer(layer, h, mix_gain, w_in_all, v_gain, w_s, b_s, w_out_all):
    B, S, D = h.shape
    tm = min(TOKEN_TILE, S)
    width = w_out_all.shape[1]
    return pl.pallas_call(
        functools.partial(_sgu_kernel, layer),
        in_specs=[_HBM_SPEC, _VMEM_SPEC, _HBM_SPEC, _VMEM_SPEC, _VMEM_SPEC, _VMEM_SPEC, _HBM_SPEC],
        out_specs=_HBM_SPEC,
        out_shape=jax.ShapeDtypeStruct((B, S, D), F32),
        scratch_shapes=[
            pltpu.VMEM((tm, width + LANES), BF16),
            pltpu.VMEM((tm, width + LANES), BF16),
            pltpu.VMEM((tm, width + LANES), BF16),
            pltpu.VMEM((SGU_GROUPS, CHUNK, CHUNK), BF16),
        ] + _weight_scratch(D, 2 * width) + _weight_scratch(width, D) + _tile_buffers(tm, D),
        compiler_params=_compiler_params(),
        name="sgu_mixer",
    )(h, mix_gain.reshape(1, D), w_in_all, v_gain.reshape(1, width), w_s, b_s.T, w_out_all)


def _ffn_kernel(layer, h_hbm, ng_ref, wup_hbm, cw_ref, cb_ref, wdown_hbm, o_hbm,
                carry_scr, act_scr, perm_scr, wup_ref, wup_stage, wup_sems, wdown_ref, wdown_stage, wdown_sems,
                hbuf, obuf, in_sems, out_sems):
    _fetch_as_bf16(wup_hbm.at[layer], wup_ref, wup_stage, wup_sems)
    _fetch_as_bf16(wdown_hbm.at[layer], wdown_ref, wdown_stage, wdown_sems)

    def tile(first_tile, h_ref, o_ref):
        _ffn_tile(first_tile, h_ref, ng_ref, cw_ref, cb_ref, o_ref, carry_scr, act_scr, perm_scr, wup_ref, wdown_ref)

    _tile_loop(h_hbm, o_hbm, hbuf, obuf, in_sems, out_sems, tile)


def _ffn_tile(first_tile, h_ref, ng_ref, cw_ref, cb_ref, o_ref, carry_scr, act_scr, perm_scr, wup_ref, wdown_ref):
    n_sub, tm, d_ff = act_scr.shape
    d = h_ref.shape[1]
    seg = tm // SUBLANES
    pitch = perm_scr.shape[2] // SUBLANES
    ct = FFN_COL_TILE
    n_col = d_ff // ct
    n_slab = d // LANES

    @pl.when(first_tile)
    def _():
        carry_scr[...] = jnp.zeros_like(carry_scr)

    first_row = lax.broadcasted_iota(jnp.int32, (seg, 2 * ct), 0) == 0
    h_pm, hn_pm, out_pm = {}, {}, {}

    def to_phase_major(i):
        for r in range(seg):
            for c in range(n_slab):
                perm_scr[i, c, pl.ds(r, SUBLANES, stride=pitch), :] = (
                    h_ref[i * tm + r * SUBLANES:i * tm + (r + 1) * SUBLANES, c * LANES:(c + 1) * LANES])
        yield
        h_pm[i] = jnp.concatenate(
            [jnp.concatenate([perm_scr[i, c, p * pitch:p * pitch + seg, :] for c in range(n_slab)], axis=1)
             for p in range(SUBLANES)], axis=0)
        hn_pm[i] = _rms_norm(h_pm[i], ng_ref[...]).astype(BF16)

    def pair(ref, rows, c):
        return jnp.concatenate([ref[rows, c * ct:(c + 1) * ct], ref[rows, d_ff + c * ct:d_ff + (c + 1) * ct]],
                               axis=1)

    def up_tile(i, c):
        c0 = 2 * ct * c
        a = jnp.dot(hn_pm[i], pair(wup_ref, slice(None), c), preferred_element_type=F32)
        yield
        wrap1 = jnp.where(first_row, carry_scr[1:2, c0:c0 + 2 * ct], pltpu.roll(a[7 * seg:8 * seg], 1, 0))
        wrap2 = jnp.where(first_row, carry_scr[0:1, c0:c0 + 2 * ct], pltpu.roll(a[6 * seg:7 * seg], 1, 0))
        carry_scr[0:1, c0:c0 + 2 * ct] = a[7 * seg - 1:7 * seg]
        carry_scr[1:2, c0:c0 + 2 * ct] = a[8 * seg - 1:8 * seg]
        a1 = jnp.concatenate([wrap1, a[0:7 * seg]], axis=0)
        a2 = jnp.concatenate([wrap2, wrap1, a[0:6 * seg]], axis=0)
        conv = (pair(cw_ref, slice(0, 1), c) * a2 + pair(cw_ref, slice(1, 2), c) * a1
                + pair(cw_ref, slice(2, 3), c) * a + pair(cb_ref, slice(None), c))
        yield
        gate = conv[:, 0:ct]
        val = conv[:, ct:2 * ct]
        act_scr[i, :, c * ct:(c + 1) * ct] = (gate / (1.0 + jnp.exp(-gate)) * val).astype(BF16)

    def down(i, n0):
        n1 = n0 + FFN_OUT_TILE
        out = h_pm[i][:, n0:n1] + jnp.dot(act_scr[i], wdown_ref[:, n0:n1], preferred_element_type=F32)
        yield
        for p in range(SUBLANES):
            for c in range(n0 // LANES, n1 // LANES):
                perm_scr[i, c, p * pitch:p * pitch + seg, :] = (
                    out[p * seg:(p + 1) * seg, c * LANES - n0:(c + 1) * LANES - n0])
        yield
        for r in range(seg):
            for c in range(n0 // LANES, n1 // LANES):
                o_ref[i * tm + r * SUBLANES:i * tm + (r + 1) * SUBLANES, c * LANES:(c + 1) * LANES] = (
                    perm_scr[i, c, pl.ds(r, SUBLANES, stride=pitch), :])

    prep = [to_phase_major(i) for i in range(n_sub)]
    ups = [[up_tile(i, c) for c in range(n_col)] for i in range(n_sub)]
    chains = []
    for i in range(n_sub):
        chains.append((prep[i], ()))
        chains.extend((ups[i][c], (prep[i],) + ((ups[i - 1][c],) if i > 0 else ())) for c in range(n_col))
        chains.extend((down(i, n0), tuple(ups[i])) for n0 in range(0, d, FFN_OUT_TILE))
    _interleave(chains, CHAINS_IN_FLIGHT)


def _ffn_layer(layer, h, norm_gain, w_up_all, conv_w, conv_b, w_down_all):
    B, S, D = h.shape
    tm = min(TOKEN_TILE, S)
    n_sub = min(FFN_TILES_PER_STEP, S // tm)
    d_ff = w_down_all.shape[1]
    return pl.pallas_call(
        functools.partial(_ffn_kernel, layer),
        in_specs=[_HBM_SPEC, _VMEM_SPEC, _HBM_SPEC, _VMEM_SPEC, _VMEM_SPEC, _HBM_SPEC],
        out_specs=_HBM_SPEC,
        out_shape=jax.ShapeDtypeStruct((B, S, D), F32),
        scratch_shapes=[
            pltpu.VMEM((SUBLANES, 2 * d_ff), F32),
            pltpu.VMEM((n_sub, tm, d_ff), BF16),
            pltpu.VMEM((n_sub, D // LANES, tm + SUBLANES * SUBLANES, LANES), F32),
        ] + _weight_scratch(D, 2 * d_ff) + _weight_scratch(d_ff, D) + _tile_buffers(n_sub * tm, D),
        compiler_params=_compiler_params(),
        name="conv_glu_ffn",
    )(h, norm_gain.reshape(1, D), w_up_all, conv_w, conv_b.reshape(1, 2 * d_ff), w_down_all)


def _band_rel_buckets():
    dist = (np.arange(CHUNK)[:, None] + CHUNK) - np.arange(2 * CHUNK)[None, :]
    n = np.maximum(dist, 0)
    max_exact = REL_BUCKETS // 2
    large = max_exact + (np.log(np.maximum(n, 1).astype(np.float32) / max_exact)
                         / math.log(REL_MAX_DIST / max_exact)
                         * (REL_BUCKETS - max_exact)).astype(np.int32)
    large = np.minimum(large, REL_BUCKETS - 1)
    bucket = np.where(n < max_exact, n, large)
    return np.where((dist >= 0) & (dist < CHUNK), bucket, -1).astype(np.int32)


def _bias_kernel(bucket_ref, rb_ref, o_ref):
    bucket = bucket_ref[...]
    for head in range(N_HEADS):
        acc = jnp.full(bucket.shape, -jnp.inf, F32)
        for b in range(REL_BUCKETS):
            acc = jnp.where(bucket == b, rb_ref[b, head], acc)
        kvh, pair, odd = head // KV_GROUP, (head % KV_GROUP) // 2, head % 2
        o_ref[kvh, odd * 2 * CHUNK:(odd + 1) * 2 * CHUNK, pair * CHUNK:(pair + 1) * CHUNK] = acc.T * LOG2_E


def _bias_table(rel_bias):
    bucket = jnp.asarray(_band_rel_buckets())
    return pl.pallas_call(
        _bias_kernel,
        in_specs=[
            pl.BlockSpec(memory_space=pltpu.VMEM),
            pl.BlockSpec(memory_space=pltpu.SMEM),
        ],
        out_specs=pl.BlockSpec(memory_space=pltpu.VMEM),
        out_shape=jax.ShapeDtypeStruct((N_KV_HEADS, 4 * CHUNK, 2 * CHUNK), F32),
        name="rel_bias_table",
    )(bucket, rel_bias)


def _head_sumsq(x, seg):
    return jnp.dot((x * x).astype(BF16), seg, preferred_element_type=F32)


def _swa_kernel(layer, h_hbm, mg_ref, wqkv_hbm, qg_ref, kg_ref, sink_ref, bias_ref, seg_ref, wo_hbm, o_hbm,
                klo_scr, khi_scr, vt_scr, q_scr, att_scr,
                wqkv_ref, wqkv_stage, wqkv_sems, wo_ref, wo_stage, wo_sems,
                hbuf, obuf, in_sems, out_sems):
    _fetch_as_bf16(wqkv_hbm.at[layer], wqkv_ref, wqkv_stage, wqkv_sems)
    _fetch_as_bf16(wo_hbm.at[layer], wo_ref, wo_stage, wo_sems)

    def tile(first_tile, h_ref, o_ref):
        _swa_tile(first_tile, h_ref, mg_ref, qg_ref, kg_ref, sink_ref, bias_ref, seg_ref, o_ref,
                  klo_scr, khi_scr, vt_scr, q_scr, att_scr, wqkv_ref, wo_ref)

    _tile_loop(h_hbm, o_hbm, hbuf, obuf, in_sems, out_sems, tile)


def _swa_tile(first_tile, h_ref, mg_ref, qg_ref, kg_ref, sink_ref, bias_ref, seg_ref, o_ref,
              klo_scr, khi_scr, vt_scr, q_scr, att_scr, wqkv_ref, wo_ref):
    tm = h_ref.shape[0]
    dq = N_HEADS * HEAD_DIM
    dkv = N_KV_HEADS * HEAD_DIM
    vt_rows = 3 * HEAD_DIM

    @pl.when(first_tile)
    def _():
        klo_scr[0:CHUNK, :] = jnp.zeros((CHUNK, klo_scr.shape[1]), BF16)
        khi_scr[0:CHUNK, :] = jnp.zeros((CHUNK, khi_scr.shape[1]), BF16)
        vt_scr[...] = jnp.zeros(vt_scr.shape, BF16)

    def store_lo_hi(lo_scr, hi_scr, x):
        zeros = jnp.zeros((tm, HEAD_DIM), F32)
        for kvh in range(N_KV_HEADS):
            x_h = x[:, kvh * HEAD_DIM:(kvh + 1) * HEAD_DIM]
            lo_scr[CHUNK:CHUNK + tm, kvh * LANES:(kvh + 1) * LANES] = (
                jnp.concatenate([x_h, zeros], axis=1).astype(BF16))
            hi_scr[CHUNK:CHUNK + tm, kvh * LANES:(kvh + 1) * LANES] = (
                jnp.concatenate([zeros, x_h], axis=1).astype(BF16))

    def store_transposed(x):
        xt = x.T
        for kvh in range(N_KV_HEADS):
            vt_scr[kvh * vt_rows + HEAD_DIM:kvh * vt_rows + 2 * HEAD_DIM, CHUNK:CHUNK + tm] = (
                xt[kvh * HEAD_DIM:(kvh + 1) * HEAD_DIM].astype(BF16))

    h = h_ref[...]
    hn = _rms_norm(h, mg_ref[...]).astype(BF16)
    seg = seg_ref[...]
    inv_hd = np.float32(1.0 / HEAD_DIM)

    def project(w0):
        xw = jnp.dot(hn, wqkv_ref[:, w0:w0 + 2 * QK_COL_TILE], preferred_element_type=F32)
        yield
        for c0 in range(w0, w0 + 2 * QK_COL_TILE, QK_COL_TILE):
            x = xw[:, c0 - w0:c0 - w0 + QK_COL_TILE]
            if c0 >= dq + dkv:
                store_transposed(x)
                yield
                continue
            r = lax.rsqrt(_head_sumsq(x, seg) * inv_hd + EPS)
            yield
            if c0 < dq:
                gain = qg_ref[:, c0:c0 + QK_COL_TILE] * (np.float32(HEAD_DIM ** -0.5) * LOG2_E)
                q_scr[:, c0:c0 + QK_COL_TILE] = (x * r * gain).astype(BF16)
            else:
                store_lo_hi(klo_scr, khi_scr, x * r * kg_ref[:, c0 - dq:c0 - dq + QK_COL_TILE])
            yield

    band_row = lax.broadcasted_iota(jnp.int32, (4 * CHUNK, 1), 0) % (2 * CHUNK)
    no_prev = jnp.where(first_tile & (band_row < CHUNK), -jnp.inf, 0.0).astype(F32)
    upper_pair = lax.broadcasted_iota(jnp.int32, (1, 2 * CHUNK), 1) >= CHUNK

    def scores(blk, kvh):
        r0 = blk * CHUNK
        c0 = kvh * KV_GROUP * HEAD_DIM
        q2 = jnp.concatenate([q_scr[r0:r0 + CHUNK, c0:c0 + LANES],
                              q_scr[r0:r0 + CHUNK, c0 + LANES:c0 + 2 * LANES]], axis=0)
        k2 = jnp.concatenate([klo_scr[r0:r0 + 2 * CHUNK, kvh * LANES:(kvh + 1) * LANES],
                              khi_scr[r0:r0 + 2 * CHUNK, kvh * LANES:(kvh + 1) * LANES]], axis=0)
        return lax.dot_general(k2, q2, (((1,), (1,)), ((), ())), preferred_element_type=F32)

    def attend(blk, kvh):
        r0 = blk * CHUNK
        c0 = kvh * KV_GROUP * HEAD_DIM
        s = scores(blk, kvh)
        yield
        s = s + bias_ref[kvh]
        if blk == 0:
            s = s + no_prev
        probs, inv_denoms = [], []
        for odd in range(2):
            t = s[odd * 2 * CHUNK:(odd + 1) * 2 * CHUNK]
            sink = LOG2_E * jnp.where(upper_pair, sink_ref[kvh * KV_GROUP + 2 + odd],
                                      sink_ref[kvh * KV_GROUP + odd])
            m = jnp.maximum(jnp.max(t, axis=0, keepdims=True), sink)
            yield
            p = jnp.exp2(t - m)
            inv_denoms.append(1.0 / (jnp.sum(p, axis=0, keepdims=True) + jnp.exp2(sink - m)))
            probs.append(p.astype(BF16))
            yield
        v0 = kvh * vt_rows
        vt2 = jnp.concatenate([vt_scr[v0 + HEAD_DIM:v0 + 3 * HEAD_DIM, r0:r0 + 2 * CHUNK],
                               vt_scr[v0:v0 + 2 * HEAD_DIM, r0:r0 + 2 * CHUNK]], axis=1)
        ot = jnp.dot(vt2, jnp.concatenate(probs, axis=0), preferred_element_type=F32)
        yield
        ot = jnp.concatenate([ot[0:HEAD_DIM] * inv_denoms[0], ot[HEAD_DIM:2 * HEAD_DIM] * inv_denoms[1]], axis=0)
        att_scr[r0:r0 + CHUNK, c0:c0 + LANES] = ot[:, 0:CHUNK].T.astype(BF16)
        att_scr[r0:r0 + CHUNK, c0 + LANES:c0 + 2 * LANES] = ot[:, CHUNK:2 * CHUNK].T.astype(BF16)

    kv_proj = project(dq)
    q_projs = [project(w0) for w0 in range(0, dq, 2 * QK_COL_TILE)]
    kvh_per_proj = 2 * QK_COL_TILE // (KV_GROUP * HEAD_DIM)
    chains = [(kv_proj, ()), (q_projs[0], ())]
    for kvh in range(N_KV_HEADS):
        chains.extend((attend(blk, kvh), (kv_proj, q_projs[kvh // kvh_per_proj])) for blk in range(tm // CHUNK))
        nxt = kvh // kvh_per_proj + 1
        if kvh % kvh_per_proj == 0 and nxt < len(q_projs):
            chains.append((q_projs[nxt], ()))
    _interleave(chains, CHAINS_IN_FLIGHT)

    kv_cols = N_KV_HEADS * LANES
    klo_scr[0:CHUNK, 0:kv_cols] = klo_scr[tm:tm + CHUNK, 0:kv_cols]
    khi_scr[0:CHUNK, 0:kv_cols] = khi_scr[tm:tm + CHUNK, 0:kv_cols]
    vt_scr[:, 0:CHUNK] = vt_scr[:, tm:tm + CHUNK]

    o_ref[...] = h + jnp.dot(att_scr[:, 0:dq], wo_ref[...], preferred_element_type=F32)


def _swa_layer(layer, h, mix_gain, w_qkv_all, q_gain, k_gain, sinks, w_o_all, bias_table):
    B, S, D = h.shape
    tm = min(TOKEN_TILE, S)
    dq = N_HEADS * HEAD_DIM
    dkv = N_KV_HEADS * HEAD_DIM
    lane_head = np.arange(QK_COL_TILE) // HEAD_DIM
    seg = jnp.asarray(lane_head[:, None] == lane_head[None, :], BF16)
    return pl.pallas_call(
        functools.partial(_swa_kernel, layer),
        in_specs=[_HBM_SPEC, _VMEM_SPEC, _HBM_SPEC, _VMEM_SPEC, _VMEM_SPEC,
                  pl.BlockSpec(memory_space=pltpu.SMEM), _VMEM_SPEC, _VMEM_SPEC, _HBM_SPEC],
        out_specs=_HBM_SPEC,
        out_shape=jax.ShapeDtypeStruct((B, S, D), F32),
        scratch_shapes=[
            pltpu.VMEM((CHUNK + tm, (N_KV_HEADS + 1) * LANES), BF16),
            pltpu.VMEM((CHUNK + tm, (N_KV_HEADS + 1) * LANES), BF16),
            pltpu.VMEM((N_KV_HEADS * 3 * HEAD_DIM, CHUNK + tm), BF16),
            pltpu.VMEM((tm, dq + LANES), BF16),
            pltpu.VMEM((tm, dq + LANES), BF16),
        ] + _weight_scratch(D, dq + 2 * dkv) + _weight_scratch(dq, D) + _tile_buffers(tm, D),
        compiler_params=_compiler_params(),
        name="swa_mixer",
    )(h, mix_gain.reshape(1, D), w_qkv_all, jnp.tile(q_gain, N_HEADS).reshape(1, dq),
      jnp.tile(k_gain, N_KV_HEADS).reshape(1, dkv), sinks, bias_table, seg, w_o_all)


def kernel(x, mix_norm, ffn_norm, sgu_w_in, sgu_v_gain, sgu_w_s, sgu_b_s, sgu_w_out, attn_w_qkv,
           attn_q_gain, attn_k_gain, attn_sinks, attn_w_o, rel_bias, ffn_w_up, ffn_conv_w, ffn_conv_b,
           ffn_w_down):
    depth = mix_norm.shape[0]
    bias_table = _bias_table(rel_bias) if depth > 1 else None
    h = x
    for i in range(depth):
        j = i // 2
        if i % 2 == 0:
            h = _sgu_layer(j, h, mix_norm[i], sgu_w_in, sgu_v_gain[j], sgu_w_s[j], sgu_b_s[j], sgu_w_out)
        else:
            h = _swa_layer(j, h, mix_norm[i], attn_w_qkv, attn_q_gain[j], attn_k_gain[j], attn_sinks[j],
                           attn_w_o, bias_table)
        h = _ffn_layer(i, h, ffn_norm[i], ffn_w_up, ffn_conv_w[i], ffn_conv_b[i], ffn_w_down)
    return h
```

```python
import functools
import math

import numpy as np
import jax
import jax.numpy as jnp
from jax import lax
from jax.experimental import pallas as pl
from jax.experimental.pallas import tpu as pltpu

F32 = jnp.float32
BF16 = jnp.bfloat16

EPS = 1e-6
LOG2_E = np.float32(1.0 / math.log(2.0))
LANES = 128
SUBLANES = 8
CHUNK = 128
HEAD_DIM = 64
N_HEADS = 16
N_KV_HEADS = 4
KV_GROUP = N_HEADS // N_KV_HEADS
SGU_GROUPS = 16
REL_BUCKETS = 32
REL_MAX_DIST = 128
CONV_WIDTH = 3

TOKEN_TILE = 512
SWA_TOKEN_TILE = 1024
FFN_TOKEN_TILE = 1024
FETCH_SLOTS = 3
FETCH_MAX_TILE_ROWS = 256
FETCH_MAX_TILE_BYTES = 3 * 512 * 1024
SGU_COL_TILE = 512
FFN_COL_TILE = 256
QK_COL_TILE = 256
CHAINS_IN_FLIGHT = 6
VMEM_LIMIT_BYTES = 56 * 1024 * 1024


def _rms_norm(x, gain):
    ms = jnp.mean(x * x, axis=-1, keepdims=True)
    return x * lax.rsqrt(ms + EPS) * gain


def _interleave(chains, in_flight):
    waiting = list(chains)
    running, finished = [], set()
    while waiting or running:
        if waiting and len(running) < in_flight and all(id(p) in finished for p in waiting[0][1]):
            running.append(waiting.pop(0)[0])
        for chain in list(running):
            if next(chain, "done") == "done":
                running.remove(chain)
                finished.add(id(chain))


def _fetch_as_bf16(src_hbm, dst_scr, stage, sems):
    n_slots, tile_rows, _ = stage.shape
    n_tiles = dst_scr.shape[0] // tile_rows

    def copy(k):
        return pltpu.make_async_copy(src_hbm.at[pl.ds(k * tile_rows, tile_rows), :],
                                     stage.at[k % n_slots], sems.at[k % n_slots])

    for k in range(min(n_slots - 1, n_tiles)):
        copy(k).start()
    for k in range(n_tiles):
        if k + n_slots - 1 < n_tiles:
            copy(k + n_slots - 1).start()
        copy(k).wait()
        dst_scr[k * tile_rows:(k + 1) * tile_rows, :] = stage[k % n_slots].astype(BF16)


def _weight_scratch(n_rows, n_cols):
    tile_rows = FETCH_MAX_TILE_ROWS
    while tile_rows * n_cols * 4 > FETCH_MAX_TILE_BYTES or n_rows % tile_rows:
        tile_rows //= 2
    assert tile_rows >= SUBLANES, (n_rows, n_cols)
    return [
        pltpu.VMEM((n_rows, n_cols), BF16),
        pltpu.VMEM((FETCH_SLOTS, tile_rows, n_cols), F32),
        pltpu.SemaphoreType.DMA((FETCH_SLOTS,)),
    ]


def _tile_loop(h_hbm, o_hbm, hbuf, obuf, in_sems, out_sems, tile_fn):
    n_batch, seq, _ = h_hbm.shape
    tm = hbuf.shape[1]
    tiles_per_seq = seq // tm
    n_tiles = n_batch * tiles_per_seq

    def hbm_tile(ref, t):
        return ref.at[t // tiles_per_seq, pl.ds(pl.multiple_of((t % tiles_per_seq) * tm, tm), tm), :]

    def in_copy(t, slot):
        return pltpu.make_async_copy(hbm_tile(h_hbm, t), hbuf.at[slot], in_sems.at[slot])

    def out_copy(t, slot):
        return pltpu.make_async_copy(obuf.at[slot], hbm_tile(o_hbm, t), out_sems.at[slot])

    in_copy(0, 0).start()

    def step(t, carry):
        slot = t % 2
        in_copy(t, slot).wait()

        @pl.when(t + 1 < n_tiles)
        def _():
            in_copy(t + 1, 1 - slot).start()

        @pl.when(t >= 2)
        def _():
            out_copy(t - 2, slot).wait()

        tile_fn(t % tiles_per_seq == 0, hbuf.at[slot], obuf.at[slot])
        out_copy(t, slot).start()
        return carry

    lax.fori_loop(0, n_tiles, step, 0)
    for t in range(max(n_tiles - 2, 0), n_tiles):
        out_copy(t, t % 2).wait()


def _tile_buffers(tm, d):
    return [
        pltpu.VMEM((2, tm, d), F32),
        pltpu.VMEM((2, tm, d), F32),
        pltpu.SemaphoreType.DMA((2,)),
        pltpu.SemaphoreType.DMA((2,)),
    ]


_VMEM_SPEC = pl.BlockSpec(memory_space=pltpu.VMEM)
_HBM_SPEC = pl.BlockSpec(memory_space=pl.ANY)


def _compiler_params():
    return pltpu.CompilerParams(vmem_limit_bytes=VMEM_LIMIT_BYTES)


def _sgu_kernel(layer, h_hbm, mg_ref, win_hbm, vg_ref, ws_ref, bs_ref, wout_hbm, o_hbm,
                u_scr, v_scr, us_scr, wsm_scr, win_ref, win_stage, win_sems, wout_ref, wout_stage, wout_sems,
                hbuf, obuf, in_sems, out_sems):
    _fetch_as_bf16(win_hbm.at[layer], win_ref, win_stage, win_sems)
    _fetch_as_bf16(wout_hbm.at[layer], wout_ref, wout_stage, wout_sems)
    row = lax.broadcasted_iota(jnp.int32, (CHUNK, CHUNK), 0)
    col = lax.broadcasted_iota(jnp.int32, (CHUNK, CHUNK), 1)
    for g in range(SGU_GROUPS):
        wsm_scr[g] = jnp.where(row >= col, ws_ref[g], 0.0).astype(BF16)

    def tile(first_tile, h_ref, o_ref):
        del first_tile
        _sgu_tile(h_ref, mg_ref, vg_ref, bs_ref, o_ref, u_scr, v_scr, us_scr, wsm_scr, win_ref, wout_ref)

    _tile_loop(h_hbm, o_hbm, hbuf, obuf, in_sems, out_sems, tile)


def _sgu_tile(h_ref, mg_ref, vg_ref, bs_ref, o_ref, u_scr, v_scr, us_scr, wsm_scr, win_ref, wout_ref):
    tm = h_ref.shape[0]
    width = wout_ref.shape[0]
    h = h_ref[...]
    hn = _rms_norm(h, mg_ref[...]).astype(BF16)

    v_sumsq = []

    def in_proj(c0):
        z = jnp.dot(hn, win_ref[:, c0:c0 + SGU_COL_TILE], preferred_element_type=F32)
        yield
        z = 0.5 * z * (1.0 + lax.erf(z * np.float32(math.sqrt(0.5))))
        if c0 < width:
            u_scr[:, c0:c0 + SGU_COL_TILE] = z
        else:
            v_sumsq.append(jnp.sum(z * z, axis=-1, keepdims=True))
            v_scr[:, c0 - width:c0 - width + SGU_COL_TILE] = z

    _interleave([(in_proj(c0), ()) for c0 in range(width, 2 * width, SGU_COL_TILE)], CHAINS_IN_FLIGHT)
    rstd = lax.rsqrt(sum(v_sumsq) * np.float32(1.0 / width) + EPS)

    n_chunk = tm // CHUNK

    def gate(g):
        g0 = g * LANES
        vn = jnp.concatenate(
            [(v_scr[cb * CHUNK:(cb + 1) * CHUNK, g0:g0 + LANES] * rstd[cb * CHUNK:(cb + 1) * CHUNK]
              * vg_ref[:, g0:g0 + LANES]).astype(BF16) for cb in range(n_chunk)], axis=1)
        yield
        s = jnp.dot(wsm_scr[g], vn, preferred_element_type=F32) + bs_ref[:, g:g + 1]
        yield
        for cb in range(n_chunk):
            r0 = cb * CHUNK
            us_scr[r0:r0 + CHUNK, g0:g0 + LANES] = (
                u_scr[r0:r0 + CHUNK, g0:g0 + LANES] * s[:, cb * CHUNK:(cb + 1) * CHUNK]).astype(BF16)

    u_projs = [in_proj(c0) for c0 in range(0, width, SGU_COL_TILE)]
    groups_per_tile = SGU_COL_TILE // LANES
    chains = [(u_projs[0], ())]
    for i, u_proj in enumerate(u_projs):
        if i + 1 < len(u_projs):
            chains.append((u_projs[i + 1], ()))
        chains.extend((gate(g), (u_proj,)) for g in range(i * groups_per_tile, (i + 1) * groups_per_tile))
    _interleave(chains, CHAINS_IN_FLIGHT)

    o_ref[...] = h + jnp.dot(us_scr[:, 0:width], wout_ref[...], preferred_element_type=F32)


def _sgu_layer(layer, h, mix_gain, w_in_all, v_gain, w_s, b_s, w_out_all):
    B, S, D = h.shape
    tm = min(TOKEN_TILE, S)
    width = w_out_all.shape[1]
    return pl.pallas_call(
        functools.partial(_sgu_kernel, layer),
        in_specs=[_HBM_SPEC, _VMEM_SPEC, _HBM_SPEC, _VMEM_SPEC, _VMEM_SPEC, _VMEM_SPEC, _HBM_SPEC],
        out_specs=_HBM_SPEC,
        out_shape=jax.ShapeDtypeStruct((B, S, D), F32),
        scratch_shapes=[
            pltpu.VMEM((tm, width + LANES), F32),
            pltpu.VMEM((tm, width + LANES), F32),
            pltpu.VMEM((tm, width + LANES), BF16),
            pltpu.VMEM((SGU_GROUPS, CHUNK, CHUNK), BF16),
        ] + _weight_scratch(D, 2 * width) + _weight_scratch(width, D) + _tile_buffers(tm, D),
        compiler_params=_compiler_params(),
        name="sgu_mixer",
    )(h, mix_gain.reshape(1, D), w_in_all, v_gain.reshape(1, width), w_s, b_s.T, w_out_all)


def _ffn_kernel(layer, h_hbm, ng_ref, wup_hbm, cw_ref, cb_ref, wdown_hbm, o_hbm,
                carry_scr, act_scr, perm_scr, wup_ref, wup_stage, wup_sems, wdown_ref, wdown_stage, wdown_sems,
                hbuf, obuf, in_sems, out_sems):
    _fetch_as_bf16(wup_hbm.at[layer], wup_ref, wup_stage, wup_sems)
    _fetch_as_bf16(wdown_hbm.at[layer], wdown_ref, wdown_stage, wdown_sems)

    def tile(first_tile, h_ref, o_ref):
        _ffn_tile(first_tile, h_ref, ng_ref, cw_ref, cb_ref, o_ref, carry_scr, act_scr, perm_scr, wup_ref, wdown_ref)

    _tile_loop(h_hbm, o_hbm, hbuf, obuf, in_sems, out_sems, tile)


def _ffn_tile(first_tile, h_ref, ng_ref, cw_ref, cb_ref, o_ref, carry_scr, act_scr, perm_scr, wup_ref, wdown_ref):
    tm, d_ff = act_scr.shape
    d = h_ref.shape[1]
    seg = tm // SUBLANES
    pitch = perm_scr.shape[1] // SUBLANES
    ct = FFN_COL_TILE
    n_slab = d // LANES

    @pl.when(first_tile)
    def _():
        carry_scr[...] = jnp.zeros_like(carry_scr)

    for r in range(seg):
        for c in range(n_slab):
            perm_scr[c, pl.ds(r, SUBLANES, stride=pitch), :] = (
                h_ref[r * SUBLANES:(r + 1) * SUBLANES, c * LANES:(c + 1) * LANES])
    h = jnp.concatenate(
        [jnp.concatenate([perm_scr[c, p * pitch:p * pitch + seg, :] for c in range(n_slab)], axis=1)
         for p in range(SUBLANES)], axis=0)
    hn = _rms_norm(h, ng_ref[...]).astype(BF16)
    first_row = lax.broadcasted_iota(jnp.int32, (seg, 2 * ct), 0) == 0

    def pair(ref, rows, c):
        return jnp.concatenate([ref[rows, c * ct:(c + 1) * ct], ref[rows, d_ff + c * ct:d_ff + (c + 1) * ct]],
                               axis=1)

    def up_tile(c):
        c0 = 2 * ct * c
        a = jnp.dot(hn, pair(wup_ref, slice(None), c), preferred_element_type=F32)
        yield
        wrap1 = jnp.where(first_row, carry_scr[1:2, c0:c0 + 2 * ct], pltpu.roll(a[7 * seg:8 * seg], 1, 0))
        wrap2 = jnp.where(first_row, carry_scr[0:1, c0:c0 + 2 * ct], pltpu.roll(a[6 * seg:7 * seg], 1, 0))
        carry_scr[0:1, c0:c0 + 2 * ct] = a[7 * seg - 1:7 * seg]
        carry_scr[1:2, c0:c0 + 2 * ct] = a[8 * seg - 1:8 * seg]
        a1 = jnp.concatenate([wrap1, a[0:7 * seg]], axis=0)
        a2 = jnp.concatenate([wrap2, wrap1, a[0:6 * seg]], axis=0)
        conv = (pair(cw_ref, slice(0, 1), c) * a2 + pair(cw_ref, slice(1, 2), c) * a1
                + pair(cw_ref, slice(2, 3), c) * a + pair(cb_ref, slice(None), c))
        yield
        gate = conv[:, 0:ct]
        val = conv[:, ct:2 * ct]
        act_scr[:, c * ct:(c + 1) * ct] = (gate / (1.0 + jnp.exp(-gate)) * val).astype(BF16)

    _interleave([(up_tile(c), ()) for c in range(d_ff // ct)], CHAINS_IN_FLIGHT)

    out = h + jnp.dot(act_scr[...], wdown_ref[...], preferred_element_type=F32)

    for p in range(SUBLANES):
        for c in range(n_slab):
            perm_scr[c, p * pitch:p * pitch + seg, :] = out[p * seg:(p + 1) * seg, c * LANES:(c + 1) * LANES]
    for r in range(seg):
        for c in range(n_slab):
            o_ref[r * SUBLANES:(r + 1) * SUBLANES, c * LANES:(c + 1) * LANES] = (
                perm_scr[c, pl.ds(r, SUBLANES, stride=pitch), :])


def _ffn_layer(layer, h, norm_gain, w_up_all, conv_w, conv_b, w_down_all):
    B, S, D = h.shape
    tm = min(FFN_TOKEN_TILE, S)
    d_ff = w_down_all.shape[1]
    return pl.pallas_call(
        functools.partial(_ffn_kernel, layer),
        in_specs=[_HBM_SPEC, _VMEM_SPEC, _HBM_SPEC, _VMEM_SPEC, _VMEM_SPEC, _HBM_SPEC],
        out_specs=_HBM_SPEC,
        out_shape=jax.ShapeDtypeStruct((B, S, D), F32),
        scratch_shapes=[
            pltpu.VMEM((SUBLANES, 2 * d_ff), F32),
            pltpu.VMEM((tm, d_ff), BF16),
            pltpu.VMEM((D // LANES, tm + SUBLANES * SUBLANES, LANES), F32),
        ] + _weight_scratch(D, 2 * d_ff) + _weight_scratch(d_ff, D) + _tile_buffers(tm, D),
        compiler_params=_compiler_params(),
        name="conv_glu_ffn",
    )(h, norm_gain.reshape(1, D), w_up_all, conv_w, conv_b.reshape(1, 2 * d_ff), w_down_all)


def _band_rel_buckets():
    dist = (np.arange(CHUNK)[:, None] + CHUNK) - np.arange(2 * CHUNK)[None, :]
    n = np.maximum(dist, 0)
    max_exact = REL_BUCKETS // 2
    large = max_exact + (np.log(np.maximum(n, 1).astype(np.float32) / max_exact)
                         / math.log(REL_MAX_DIST / max_exact)
                         * (REL_BUCKETS - max_exact)).astype(np.int32)
    large = np.minimum(large, REL_BUCKETS - 1)
    bucket = np.where(n < max_exact, n, large)
    return np.where((dist >= 0) & (dist < CHUNK), bucket, -1).astype(np.int32)


def _bias_kernel(bucket_ref, rb_ref, o_ref):
    bucket = bucket_ref[...]
    for head in range(N_HEADS):
        acc = jnp.full(bucket.shape, -jnp.inf, F32)
        for b in range(REL_BUCKETS):
            acc = jnp.where(bucket == b, rb_ref[b, head], acc)
        kvh, pair, odd = head // KV_GROUP, (head % KV_GROUP) // 2, head % 2
        o_ref[kvh, odd * 2 * CHUNK:(odd + 1) * 2 * CHUNK, pair * CHUNK:(pair + 1) * CHUNK] = acc.T * LOG2_E


def _bias_table(rel_bias):
    bucket = jnp.asarray(_band_rel_buckets())
    return pl.pallas_call(
        _bias_kernel,
        in_specs=[
            pl.BlockSpec(memory_space=pltpu.VMEM),
            pl.BlockSpec(memory_space=pltpu.SMEM),
        ],
        out_specs=pl.BlockSpec(memory_space=pltpu.VMEM),
        out_shape=jax.ShapeDtypeStruct((N_KV_HEADS, 4 * CHUNK, 2 * CHUNK), F32),
        name="rel_bias_table",
    )(bucket, rel_bias)


def _head_sumsq(x, seg):
    return jnp.dot((x * x).astype(BF16), seg, preferred_element_type=F32)


def _swa_kernel(layer, h_hbm, mg_ref, wqkv_hbm, qg_ref, kg_ref, sink_ref, bias_ref, seg_ref, wo_hbm, o_hbm,
                klo_scr, khi_scr, vt_scr, q_scr, att_scr,
                wqkv_ref, wqkv_stage, wqkv_sems, wo_ref, wo_stage, wo_sems,
                hbuf, obuf, in_sems, out_sems):
    _fetch_as_bf16(wqkv_hbm.at[layer], wqkv_ref, wqkv_stage, wqkv_sems)
    _fetch_as_bf16(wo_hbm.at[layer], wo_ref, wo_stage, wo_sems)

    def tile(first_tile, h_ref, o_ref):
        _swa_tile(first_tile, h_ref, mg_ref, qg_ref, kg_ref, sink_ref, bias_ref, seg_ref, o_ref,
                  klo_scr, khi_scr, vt_scr, q_scr, att_scr, wqkv_ref, wo_ref)

    _tile_loop(h_hbm, o_hbm, hbuf, obuf, in_sems, out_sems, tile)


def _swa_tile(first_tile, h_ref, mg_ref, qg_ref, kg_ref, sink_ref, bias_ref, seg_ref, o_ref,
              klo_scr, khi_scr, vt_scr, q_scr, att_scr, wqkv_ref, wo_ref):
    tm = h_ref.shape[0]
    dq = N_HEADS * HEAD_DIM
    dkv = N_KV_HEADS * HEAD_DIM
    vt_rows = 3 * HEAD_DIM

    @pl.when(first_tile)
    def _():
        klo_scr[0:CHUNK, :] = jnp.zeros((CHUNK, klo_scr.shape[1]), BF16)
        khi_scr[0:CHUNK, :] = jnp.zeros((CHUNK, khi_scr.shape[1]), BF16)
        vt_scr[...] = jnp.zeros(vt_scr.shape, BF16)

    def store_lo_hi(lo_scr, hi_scr, x):
        zeros = jnp.zeros((tm, HEAD_DIM), F32)
        for kvh in range(N_KV_HEADS):
            x_h = x[:, kvh * HEAD_DIM:(kvh + 1) * HEAD_DIM]
            lo_scr[CHUNK:CHUNK + tm, kvh * LANES:(kvh + 1) * LANES] = (
                jnp.concatenate([x_h, zeros], axis=1).astype(BF16))
            hi_scr[CHUNK:CHUNK + tm, kvh * LANES:(kvh + 1) * LANES] = (
                jnp.concatenate([zeros, x_h], axis=1).astype(BF16))

    def store_transposed(x):
        xt = x.T
        for kvh in range(N_KV_HEADS):
            vt_scr[kvh * vt_rows + HEAD_DIM:kvh * vt_rows + 2 * HEAD_DIM, CHUNK:CHUNK + tm] = (
                xt[kvh * HEAD_DIM:(kvh + 1) * HEAD_DIM].astype(BF16))

    h = h_ref[...]
    hn = _rms_norm(h, mg_ref[...]).astype(BF16)
    seg = seg_ref[...]
    inv_hd = np.float32(1.0 / HEAD_DIM)

    def project(w0):
        xw = jnp.dot(hn, wqkv_ref[:, w0:w0 + 2 * QK_COL_TILE], preferred_element_type=F32)
        yield
        for c0 in range(w0, w0 + 2 * QK_COL_TILE, QK_COL_TILE):
            x = xw[:, c0 - w0:c0 - w0 + QK_COL_TILE]
            if c0 >= dq + dkv:
                store_transposed(x)
                yield
                continue
            r = lax.rsqrt(_head_sumsq(x, seg) * inv_hd + EPS)
            yield
            if c0 < dq:
                gain = qg_ref[:, c0:c0 + QK_COL_TILE] * (np.float32(HEAD_DIM ** -0.5) * LOG2_E)
                q_scr[:, c0:c0 + QK_COL_TILE] = (x * r * gain).astype(BF16)
            else:
                store_lo_hi(klo_scr, khi_scr, x * r * kg_ref[:, c0 - dq:c0 - dq + QK_COL_TILE])
            yield

    band_row = lax.broadcasted_iota(jnp.int32, (4 * CHUNK, 1), 0) % (2 * CHUNK)
    no_prev = jnp.where(first_tile & (band_row < CHUNK), -jnp.inf, 0.0).astype(F32)
    upper_pair = lax.broadcasted_iota(jnp.int32, (1, 2 * CHUNK), 1) >= CHUNK

    def scores(blk, kvh):
        r0 = blk * CHUNK
        c0 = kvh * KV_GROUP * HEAD_DIM
        q2 = jnp.concatenate([q_scr[r0:r0 + CHUNK, c0:c0 + LANES],
                              q_scr[r0:r0 + CHUNK, c0 + LANES:c0 + 2 * LANES]], axis=0)
        k2 = jnp.concatenate([klo_scr[r0:r0 + 2 * CHUNK, kvh * LANES:(kvh + 1) * LANES],
                              khi_scr[r0:r0 + 2 * CHUNK, kvh * LANES:(kvh + 1) * LANES]], axis=0)
        return lax.dot_general(k2, q2, (((1,), (1,)), ((), ())), preferred_element_type=F32)

    def attend(blk, kvh):
        r0 = blk * CHUNK
        c0 = kvh * KV_GROUP * HEAD_DIM
        s = scores(blk, kvh)
        yield
        s = s + bias_ref[kvh]
        if blk == 0:
            s = s + no_prev
        probs, inv_denoms = [], []
        for odd in range(2):
            t = s[odd * 2 * CHUNK:(odd + 1) * 2 * CHUNK]
            sink = LOG2_E * jnp.where(upper_pair, sink_ref[kvh * KV_GROUP + 2 + odd],
                                      sink_ref[kvh * KV_GROUP + odd])
            m = jnp.maximum(jnp.max(t, axis=0, keepdims=True), sink)
            yield
            p = jnp.exp2(t - m)
            inv_denoms.append(1.0 / (jnp.sum(p, axis=0, keepdims=True) + jnp.exp2(sink - m)))
            probs.append(p.astype(BF16))
            yield
        v0 = kvh * vt_rows
        vt2 = jnp.concatenate([vt_scr[v0 + HEAD_DIM:v0 + 3 * HEAD_DIM, r0:r0 + 2 * CHUNK],
                               vt_scr[v0:v0 + 2 * HEAD_DIM, r0:r0 + 2 * CHUNK]], axis=1)
        ot = jnp.dot(vt2, jnp.concatenate(probs, axis=0), preferred_element_type=F32)
        yield
        ot = jnp.concatenate([ot[0:HEAD_DIM] * inv_denoms[0], ot[HEAD_DIM:2 * HEAD_DIM] * inv_denoms[1]], axis=0)
        att_scr[r0:r0 + CHUNK, c0:c0 + LANES] = ot[:, 0:CHUNK].T.astype(BF16)
        att_scr[r0:r0 + CHUNK, c0 + LANES:c0 + 2 * LANES] = ot[:, CHUNK:2 * CHUNK].T.astype(BF16)

    kv_proj = project(dq)
    q_projs = [project(w0) for w0 in range(0, dq, 2 * QK_COL_TILE)]
    kvh_per_proj = 2 * QK_COL_TILE // (KV_GROUP * HEAD_DIM)
    chains = [(kv_proj, ()), (q_projs[0], ())]
    for kvh in range(N_KV_HEADS):
        chains.extend((attend(blk, kvh), (kv_proj, q_projs[kvh // kvh_per_proj])) for blk in range(tm // CHUNK))
        nxt = kvh // kvh_per_proj + 1
        if kvh % kvh_per_proj == 0 and nxt < len(q_projs):
            chains.append((q_projs[nxt], ()))
    _interleave(chains, CHAINS_IN_FLIGHT)

    kv_cols = N_KV_HEADS * LANES
    klo_scr[0:CHUNK, 0:kv_cols] = klo_scr[tm:tm + CHUNK, 0:kv_cols]
    khi_scr[0:CHUNK, 0:kv_cols] = khi_scr[tm:tm + CHUNK, 0:kv_cols]
    vt_scr[:, 0:CHUNK] = vt_scr[:, tm:tm + CHUNK]

    o_ref[...] = h + jnp.dot(att_scr[:, 0:dq], wo_ref[...], preferred_element_type=F32)


def _swa_layer(layer, h, mix_gain, w_qkv_all, q_gain, k_gain, sinks, w_o_all, bias_table):
    B, S, D = h.shape
    tm = min(SWA_TOKEN_TILE, S)
    dq = N_HEADS * HEAD_DIM
    dkv = N_KV_HEADS * HEAD_DIM
    lane_head = np.arange(QK_COL_TILE) // HEAD_DIM
    seg = jnp.asarray(lane_head[:, None] == lane_head[None, :], BF16)
    return pl.pallas_call(
        functools.partial(_swa_kernel, layer),
        in_specs=[_HBM_SPEC, _VMEM_SPEC, _HBM_SPEC, _VMEM_SPEC, _VMEM_SPEC,
                  pl.BlockSpec(memory_space=pltpu.SMEM), _VMEM_SPEC, _VMEM_SPEC, _HBM_SPEC],
        out_specs=_HBM_SPEC,
        out_shape=jax.ShapeDtypeStruct((B, S, D), F32),
        scratch_shapes=[
            pltpu.VMEM((CHUNK + tm, (N_KV_HEADS + 1) * LANES), BF16),
            pltpu.VMEM((CHUNK + tm, (N_KV_HEADS + 1) * LANES), BF16),
            pltpu.VMEM((N_KV_HEADS * 3 * HEAD_DIM, CHUNK + tm), BF16),
            pltpu.VMEM((tm, dq + LANES), BF16),
            pltpu.VMEM((tm, dq + LANES), BF16),
        ] + _weight_scratch(D, dq + 2 * dkv) + _weight_scratch(dq, D) + _tile_buffers(tm, D),
        compiler_params=_compiler_params(),
        name="swa_mixer",
    )(h, mix_gain.reshape(1, D), w_qkv_all, jnp.tile(q_gain, N_HEADS).reshape(1, dq),
      jnp.tile(k_gain, N_KV_HEADS).reshape(1, dkv), sinks, bias_table, seg, w_o_all)


def kernel(x, mix_norm, ffn_norm, sgu_w_in, sgu_v_gain, sgu_w_s, sgu_b_s, sgu_w_out, attn_w_qkv,
           attn_q_gain, attn_k_gain, attn_sinks, attn_w_o, rel_bias, ffn_w_up, ffn_conv_w, ffn_conv_b,
           ffn_w_down):
    depth = mix_norm.shape[0]
    bias_table = _bias_table(rel_bias) if depth > 1 else None
    h = x
    for i in range(depth):
        j = i // 2
        if i % 2 == 0:
            h = _sgu_layer(j, h, mix_norm[i], sgu_w_in, sgu_v_gain[j], sgu_w_s[j], sgu_b_s[j], sgu_w_out)
        else:
            h = _swa_layer(j, h, mix_norm[i], attn_w_qkv, attn_q_gain[j], attn_k_gain[j], attn_sinks[j],
                           attn_w_o, bias_table)
        h = _ffn_layer(i, h, ffn_norm[i], ffn_w_up, ffn_conv_w[i], ffn_conv_b[i], ffn_w_down)
    return h
```
